```python
import jax, jax.numpy as jnp
from jax import lax
import numpy as np

D_MODEL = 1024
BATCH = 4
SEQ = 4096
DEPTH = 2

GRID_W = 64
N_MEM = 256
HEAD_DIM = 64
NA_HEADS = 6
NA_KH = 8
NA_KW = 16
DIL_PAIRS = ((128, 1), (512, 4), (2048, 16))
DIL_HEADS_PER_GROUP = 2
DIL_HEADS = 6
MEM_HEADS = 4
D_A = 384
D_B = 384
D_B_OUT = 128
D_M = 256
N_BRANCH = 3
D_IN = 3 * D_A + 3 * D_B + D_M + N_BRANCH * D_MODEL
D_FF = 2816
CONV_W = 3
Q_BLOCK = 128
RMS_EPS = 1e-6
NEG_INF = -1e30

kernel_name = 'hybrid_na_dilated_mem_encoder'


def rms_norm(x, g):
    xf = x.astype(jnp.float32)
    y = xf * lax.rsqrt(jnp.mean(xf * xf, axis=-1, keepdims=True) + RMS_EPS)
    return (y * g.astype(jnp.float32)).astype(x.dtype)


def split_heads(t, n_heads):
    B, S, _ = t.shape
    return t.reshape(B, S, n_heads, HEAD_DIM).transpose(0, 2, 1, 3)


def alibi_slopes(n):
    return 2.0 ** (-8.0 * jnp.arange(1, n + 1, dtype=jnp.float32) / n)


def neighborhood_attention(q, k, v, rpb):
    B, H, S, d = q.shape
    rows = S // GRID_W
    kh = min(NA_KH, rows)
    kw = NA_KW
    scale = d ** -0.5
    qg = q.reshape(B, H, rows, GRID_W, d)
    kg = k.reshape(B, H, rows, GRID_W, d)
    vg = v.reshape(B, H, rows, GRID_W, d)
    cols = jnp.arange(GRID_W)
    col_start = jnp.clip(cols - kw // 2, 0, GRID_W - kw)
    col_idx = col_start[:, None] + jnp.arange(kw)[None, :]
    col_off = col_idx - cols[:, None] + (NA_KW - 1)
    rpb_c = rpb[:, :, col_off]

    def one_row(i):
        r0 = jnp.clip(i - kh // 2, 0, rows - kh)
        k_band = lax.dynamic_slice_in_dim(kg, r0, kh, axis=2)
        v_band = lax.dynamic_slice_in_dim(vg, r0, kh, axis=2)
        k_win = k_band[:, :, :, col_idx, :]
        v_win = v_band[:, :, :, col_idx, :]
        q_row = lax.dynamic_index_in_dim(qg, i, axis=2, keepdims=False)
        s = jnp.einsum('bhqd,bhrqcd->bhqrc', q_row, k_win).astype(jnp.float32) * scale
        row_off = r0 + jnp.arange(kh) - i + (NA_KH - 1)
        bias = jnp.take(rpb_c, row_off, axis=1).transpose(0, 2, 1, 3)
        s = s + bias.astype(jnp.float32)[None]
        p = jax.nn.softmax(s.reshape(B, H, GRID_W, kh * kw), axis=-1).reshape(B, H, GRID_W, kh, kw)
        return jnp.einsum('bhqrc,bhrqcd->bhqd', p.astype(v.dtype), v_win)

    out = lax.map(one_row, jnp.arange(rows))
    return out.transpose(1, 2, 0, 3, 4).reshape(B, H, S, d)


def dilated_attention(q, k, v):
    B, _, S, d = q.shape
    n_blk = S // Q_BLOCK
    scale = d ** -0.5
    slopes = alibi_slopes(DIL_HEADS)
    outs, lses = [], []
    for g, (window, dilation) in enumerate(DIL_PAIRS):
        hs = slice(g * DIL_HEADS_PER_GROUP, (g + 1) * DIL_HEADS_PER_GROUP)
        qg, kg, vg = q[:, hs], k[:, hs], v[:, hs]
        half = window // 2 // dilation
        offs = dilation * jnp.arange(-half, half + 1)
        alibi = -slopes[hs][:, None, None] * jnp.abs(offs).astype(jnp.float32)[None, None, :]
        qb = qg.reshape(B, DIL_HEADS_PER_GROUP, n_blk, Q_BLOCK, d)

        def one_block(bi):
            pos = bi * Q_BLOCK + jnp.arange(Q_BLOCK)
            idx = pos[:, None] + offs[None, :]
            valid = (idx >= 0) & (idx < S)
            idx_c = jnp.clip(idx, 0, S - 1)
            kk = jnp.take(kg, idx_c, axis=2)
            vv = jnp.take(vg, idx_c, axis=2)
            qq = lax.dynamic_index_in_dim(qb, bi, axis=2, keepdims=False)
            s = jnp.einsum('bhqd,bhqkd->bhqk', qq, kk).astype(jnp.float32) * scale + alibi[None]
            s = jnp.where(valid[None, None], s, NEG_INF)
            m = jnp.max(s, axis=-1, keepdims=True)
            p = jnp.exp(s - m)
            den = jnp.sum(p, axis=-1, keepdims=True)
            o = jnp.einsum('bhqk,bhqkd->bhqd', (p / den).astype(v.dtype), vv)
            lse = (m + jnp.log(den))[..., 0]
            return o, lse

        o, lse = lax.map(one_block, jnp.arange(n_blk))
        outs.append(o.transpose(1, 2, 0, 3, 4).reshape(B, DIL_HEADS_PER_GROUP, S, d))
        lses.append(lse.transpose(1, 2, 0, 3).reshape(B, DIL_HEADS_PER_GROUP, S))
    o_all = jnp.stack(outs, axis=0)
    alpha = jax.nn.softmax(jnp.stack(lses, axis=0), axis=0)
    out = jnp.einsum('gbhs,gbhsd->bshd', alpha.astype(o_all.dtype), o_all)
    return out.reshape(B, S, D_B_OUT)


def memory_attention(q, mem_n, w_kv):
    B, S, _ = q.shape
    kv = mem_n @ w_kv
    km, vm = jnp.split(kv, 2, axis=-1)
    qh = q.reshape(B, S, MEM_HEADS, HEAD_DIM)
    km = km.reshape(B, -1, MEM_HEADS, HEAD_DIM)
    vm = vm.reshape(B, -1, MEM_HEADS, HEAD_DIM)
    s = jnp.einsum('bshd,bmhd->bhsm', qh, km).astype(jnp.float32) * (HEAD_DIM ** -0.5)
    p = jax.nn.softmax(s, axis=-1)
    o = jnp.einsum('bhsm,bmhd->bshd', p.astype(vm.dtype), vm)
    return o.reshape(B, S, D_M)


def depthwise_conv_centred(u, w, b):
    S = u.shape[1]
    pad = CONV_W // 2
    up = jnp.pad(u, ((0, 0), (pad, pad), (0, 0)))
    y = b
    for j in range(CONV_W):
        y = y + up[:, j:j + S] * w[j]
    return y


def setup_inputs(seed: int = 0) -> dict:
    key = jax.random.key(seed)
    ks = jax.random.split(key, 20)
    f32 = jnp.float32

    def nrm(k, shape, scale):
        return jax.random.normal(k, shape, f32) * scale

    def gain(k, shape):
        return 1.0 + 0.05 * jax.random.normal(k, shape, f32)

    return {
        'x': nrm(ks[0], (BATCH, SEQ, D_MODEL), 1.0),
        'mem': nrm(ks[1], (BATCH, N_MEM, D_MODEL), 1.0),
        'mem_norm_g': gain(ks[2], (D_MODEL,)),
        'g_pre_mix': gain(ks[3], (DEPTH, D_MODEL)),
        'w_in': nrm(ks[4], (DEPTH, D_MODEL, D_IN), D_MODEL ** -0.5),
        'rpb_na': nrm(ks[5], (DEPTH, NA_HEADS, 2 * NA_KH - 1, 2 * NA_KW - 1), 0.5),
        'w_mem_kv': nrm(ks[6], (DEPTH, D_MODEL, 2 * D_M), D_MODEL ** -0.5),
        'b_gate': nrm(ks[7], (DEPTH, N_BRANCH, D_MODEL), 0.1),
        'w_br_a': nrm(ks[8], (DEPTH, D_A, D_MODEL), D_A ** -0.5),
        'w_br_b': nrm(ks[9], (DEPTH, D_B_OUT, D_MODEL), D_B_OUT ** -0.5),
        'w_br_m': nrm(ks[10], (DEPTH, D_M, D_MODEL), D_M ** -0.5),
        'w_out': nrm(ks[11], (DEPTH, D_MODEL, D_MODEL), D_MODEL ** -0.5),
        'g_post_mix': gain(ks[12], (DEPTH, D_MODEL)),
        'g_pre_ffn': gain(ks[13], (DEPTH, D_MODEL)),
        'w_up': nrm(ks[14], (DEPTH, D_MODEL, 2 * D_FF), D_MODEL ** -0.5),
        'conv_w': nrm(ks[15], (DEPTH, CONV_W, 2 * D_FF), CONV_W ** -0.5),
        'conv_b': nrm(ks[16], (DEPTH, 2 * D_FF), 0.02),
        'w_down': nrm(ks[17], (DEPTH, D_FF, D_MODEL), D_FF ** -0.5),
        'g_post_ffn': gain(ks[18], (DEPTH, D_MODEL)),
    }


def reference(x, mem, mem_norm_g, g_pre_mix, w_in, rpb_na, w_mem_kv, b_gate, w_br_a, w_br_b,
              w_br_m, w_out, g_post_mix, g_pre_ffn, w_up, conv_w, conv_b, w_down, g_post_ffn):
    B, S, D = x.shape
    splits = [D_A, 2 * D_A, 3 * D_A, 3 * D_A + D_B, 3 * D_A + 2 * D_B, 3 * D_A + 3 * D_B,
              3 * D_A + 3 * D_B + D_M]
    mem_n = rms_norm(mem, mem_norm_g)
    for l in range(DEPTH):
        h = rms_norm(x, g_pre_mix[l])
        proj = h @ w_in[l]
        q_a, k_a, v_a, q_b, k_b, v_b, q_m, gate_pre = jnp.split(proj, splits, axis=-1)
        o_a = neighborhood_attention(split_heads(q_a, NA_HEADS), split_heads(k_a, NA_HEADS),
                                     split_heads(v_a, NA_HEADS), rpb_na[l])
        o_a = o_a.transpose(0, 2, 1, 3).reshape(B, S, D_A)
        o_b = dilated_attention(split_heads(q_b, DIL_HEADS), split_heads(k_b, DIL_HEADS),
                                split_heads(v_b, DIL_HEADS))
        o_m = memory_attention(q_m, mem_n, w_mem_kv[l])
        gates = jax.nn.sigmoid(gate_pre.reshape(B, S, N_BRANCH, D) + b_gate[l])
        merged = (gates[:, :, 0] * (o_a @ w_br_a[l])
                  + gates[:, :, 1] * (o_b @ w_br_b[l])
                  + gates[:, :, 2] * (o_m @ w_br_m[l]))
        x = x + rms_norm(merged @ w_out[l], g_post_mix[l])
        h = rms_norm(x, g_pre_ffn[l])
        u = depthwise_conv_centred(h @ w_up[l], conv_w[l], conv_b[l])
        a, b = jnp.split(u, 2, axis=-1)
        f = jax.nn.gelu(a) * b
        x = x + rms_norm(f @ w_down[l], g_post_ffn[l])
    return x
```

```python
import functools
import math

import jax
import jax.numpy as jnp
from jax import lax
from jax.experimental import pallas as pl
from jax.experimental.pallas import tpu as pltpu

D_MODEL = 1024
GRID_W = 64
N_MEM = 256
HEAD_DIM = 64
NA_HEADS = 6
NA_KH = 8
NA_KW = 16
DIL_PAIRS = ((128, 1), (512, 4), (2048, 16))
DIL_HEADS = 6
MEM_HEADS = 4
D_A = 384
D_B = 384
D_B_OUT = 128
D_M = 256
N_BRANCH = 3
D_IN = 3 * D_A + 3 * D_B + D_M + N_BRANCH * D_MODEL
D_FF = 2816
RMS_EPS = 1e-6
NEG_INF = -1e30

LANES = 128
HALO = 16
VMEM_LIMIT = 56 * 1024 * 1024

F32 = jnp.float32
BF16 = jnp.bfloat16


def _rms(x, g):
    return x * lax.rsqrt(jnp.mean(x * x, axis=-1, keepdims=True) + RMS_EPS) * g


def _dot(a, b):
    return jnp.dot(a, b, preferred_element_type=F32)


def _dot_nt(a, b):
    return lax.dot_general(a, b, (((1,), (1,)), ((), ())), preferred_element_type=F32)


def _resident(shape):
    n = len(shape)
    return pl.BlockSpec(shape, lambda *_: (0,) * n, pipeline_mode=pl.Buffered(1))


def _params():
    return pltpu.CompilerParams(dimension_semantics=("arbitrary",), vmem_limit_bytes=VMEM_LIMIT)


def _params2():
    return pltpu.CompilerParams(dimension_semantics=("arbitrary", "arbitrary"),
                                vmem_limit_bytes=VMEM_LIMIT)


def _memkv_kernel(mem_ref, g_ref, w_ref, o_ref):
    mn = _rms(mem_ref[0], g_ref[...]).astype(BF16)
    o_ref[0, 0] = _dot(mn, w_ref[0]).astype(BF16)


def _memkv(mem, g, w_kv):
    depth = w_kv.shape[0]
    b = mem.shape[0]
    return pl.pallas_call(
        _memkv_kernel,
        grid=(depth, b),
        in_specs=[
            pl.BlockSpec((1, N_MEM, D_MODEL), lambda l, i: (i, 0, 0)),
            pl.BlockSpec((1, D_MODEL), lambda l, i: (0, 0)),
            pl.BlockSpec((1, D_MODEL, 2 * D_M), lambda l, i: (l, 0, 0)),
        ],
        out_specs=pl.BlockSpec((1, 1, N_MEM, 2 * D_M), lambda l, i: (l, i, 0, 0)),
        out_shape=jax.ShapeDtypeStruct((depth, b, N_MEM, 2 * D_M), BF16),
        compiler_params=_params2(),
        name="memkv",
    )(mem, g, w_kv)


IN_TM = 512
IN_CHUNK = 512
_SLABS = D_IN // LANES
_A_SLABS = D_A // LANES
_M_SLABS = D_M // LANES


def _inproj_kernel(x_ref, g_ref, w_ref, bg_ref, qa_ref, ka_ref, va_ref, b0_ref, b1_ref, b2_ref,
                   qm_ref, gate_ref):
    h = _rms(x_ref[...], g_ref[...]).astype(BF16)
    scale = HEAD_DIM ** -0.5
    b_refs = (b0_ref, b1_ref, b2_ref)

    def put(slab, val):
        s = slab
        if s < 3 * _A_SLABS:
            which, j = divmod(s, _A_SLABS)
            ref = (qa_ref, ka_ref, va_ref)[which]
            if which == 0:
                val = val * scale
            ref[:, j * LANES:(j + 1) * LANES] = val.astype(BF16)
            return
        s -= 3 * _A_SLABS
        if s < 9:
            which, grp = divmod(s, 3)
            if which == 0:
                val = val * scale
            b_refs[grp][:, which * LANES:(which + 1) * LANES] = val.astype(BF16)
            return
        s -= 9
        if s < _M_SLABS:
            qm_ref[:, s * LANES:(s + 1) * LANES] = (val * scale).astype(BF16)
            return
        s -= _M_SLABS
        gate = jax.nn.sigmoid(val + bg_ref[:, s * LANES:(s + 1) * LANES])
        gate_ref[:, s * LANES:(s + 1) * LANES] = gate.astype(BF16)

    per = IN_CHUNK // LANES
    for c in range(D_IN // IN_CHUNK):
        r = _dot(h, w_ref[:, c * IN_CHUNK:(c + 1) * IN_CHUNK])
        for j in range(per):
            put(c * per + j, r[:, j * LANES:(j + 1) * LANES])


def _inproj(x2, g, w_in, b_gate):
    t = x2.shape[0]
    row = lambda i: (i, 0)
    outs = [
        jax.ShapeDtypeStruct((t, D_A), BF16), jax.ShapeDtypeStruct((t, D_A), BF16),
        jax.ShapeDtypeStruct((t, D_A), BF16),
        jax.ShapeDtypeStruct((t, 3 * LANES), BF16), jax.ShapeDtypeStruct((t, 3 * LANES), BF16),
        jax.ShapeDtypeStruct((t, 3 * LANES), BF16),
        jax.ShapeDtypeStruct((t, D_M), BF16),
        jax.ShapeDtypeStruct((t, N_BRANCH * D_MODEL), BF16),
    ]
    return pl.pallas_call(
        _inproj_kernel,
        grid=(t // IN_TM,),
        in_specs=[
            pl.BlockSpec((IN_TM, D_MODEL), row),
            _resident((1, D_MODEL)),
            _resident((D_MODEL, D_IN)),
            _resident((1, N_BRANCH * D_MODEL)),
        ],
        out_specs=[pl.BlockSpec((IN_TM, o.shape[1]), row) for o in outs],
        out_shape=outs,
        compiler_params=_params(),
        name="inproj",
    )(x2, g, w_in, b_gate)


NA_ROWS_PER_STEP = 8
NA_BAND = NA_KH * GRID_W


def _na_bias_table(rpb, rows):
    kh = min(NA_KH, rows)
    cols = jnp.arange(GRID_W)
    c0 = jnp.clip(cols - NA_KW // 2, 0, GRID_W - NA_KW)
    cc = cols[None, :]
    valid = (cc >= c0[:, None]) & (cc < c0[:, None] + NA_KW)
    col_off = jnp.clip(cc - cols[:, None] + (NA_KW - 1), 0, 2 * NA_KW - 2)
    row_off = jnp.arange(kh)[None, :] + jnp.arange(kh)[:, None]
    tab = rpb[:, row_off][:, :, :, col_off]
    tab = jnp.where(valid[None, None, None], tab.astype(F32), NEG_INF)
    return tab.transpose(0, 1, 3, 2, 4).reshape(rpb.shape[0], kh, GRID_W, kh * GRID_W)


def _half_select(lane, a, b):
    return jnp.where(lane < HEAD_DIM, a, b)


def _na_kernel(q_ref, k_ref, v_ref, tab_ref, o_ref, *, rows):
    rb = pl.program_id(1)
    lane = lax.broadcasted_iota(jnp.int32, (GRID_W, LANES), 1)
    zero = jnp.zeros((GRID_W, LANES), BF16)

    def row_body(lr, carry):
        i = rb * NA_ROWS_PER_STEP + lr
        r0 = jnp.clip(i - NA_KH // 2, 0, rows - NA_KH)
        dl = r0 - i + (NA_KH - 1)
        ks = pl.multiple_of(r0 * GRID_W, GRID_W)
        qs = pl.multiple_of(lr * GRID_W, GRID_W)
        for hp in range(NA_HEADS // 2):
            cs = slice(hp * LANES, (hp + 1) * LANES)
            q2 = q_ref[0, pl.ds(qs, GRID_W), cs]
            k2 = k_ref[0, pl.ds(ks, NA_BAND), cs]
            v2 = v_ref[0, pl.ds(ks, NA_BAND), cs]
            outs = []
            for hh in range(2):
                qh = jnp.where(lane < HEAD_DIM, q2, zero) if hh == 0 else \
                    jnp.where(lane < HEAD_DIM, zero, q2)
                s = _dot_nt(qh, k2) + tab_ref[2 * hp + hh, dl]
                m = jnp.max(s, axis=-1, keepdims=True)
                p = jnp.exp(s - m)
                den = jnp.sum(p, axis=-1, keepdims=True)
                outs.append(_dot(p.astype(BF16), v2) / den)
            o_ref[0, pl.ds(qs, GRID_W), cs] = _half_select(lane, outs[0], outs[1]).astype(BF16)
        return carry

    lax.fori_loop(0, NA_ROWS_PER_STEP, row_body, 0)


def _na_attention(q, k, v, tab, b, s):
    rows = s // GRID_W
    qblk = NA_ROWS_PER_STEP * GRID_W
    q3, k3, v3 = (a.reshape(b, s, D_A) for a in (q, k, v))
    out = pl.pallas_call(
        functools.partial(_na_kernel, rows=rows),
        grid=(b, rows // NA_ROWS_PER_STEP),
        in_specs=[
            pl.BlockSpec((1, qblk, D_A), lambda i, j: (i, j, 0)),
            pl.BlockSpec((1, s, D_A), lambda i, j: (i, 0, 0)),
            pl.BlockSpec((1, s, D_A), lambda i, j: (i, 0, 0)),
            _resident(tab.shape),
        ],
        out_specs=pl.BlockSpec((1, qblk, D_A), lambda i, j: (i, j, 0)),
        out_shape=jax.ShapeDtypeStruct((b, s, D_A), BF16),
        compiler_params=_params2(),
        name="na_attn",
    )(q3, k3, v3, tab)
    return out.reshape(b * s, D_A)


DIL_QB = 128
DIL_HALF = 64
DIL_KB = DIL_QB + 2 * DIL_HALF


def _alibi_slope(head):
    return 2.0 ** (-8.0 * (head + 1) / DIL_HEADS)


def _dil_kernel(x_ref, o_ref, lse_ref, *, length, dilation, slopes):
    lane = lax.broadcasted_iota(jnp.int32, (DIL_QB, LANES), 1)
    zero = jnp.zeros((DIL_QB, LANES), BF16)
    rel0 = (lax.broadcasted_iota(jnp.int32, (DIL_QB, DIL_KB), 1)
            - lax.broadcasted_iota(jnp.int32, (DIL_QB, DIL_KB), 0))

    def blk(bi, carry):
        q0 = pl.multiple_of(bi * DIL_QB, DIL_QB)
        ks = pl.multiple_of(jnp.clip(q0 - DIL_HALF, 0, length - DIL_KB), DIL_HALF)
        q2 = x_ref[0, pl.ds(q0, DIL_QB), 0:LANES]
        k2 = x_ref[0, pl.ds(ks, DIL_KB), LANES:2 * LANES]
        v2 = x_ref[0, pl.ds(ks, DIL_KB), 2 * LANES:3 * LANES]
        dist = jnp.abs(rel0 + (ks - q0))
        valid = dist <= DIL_HALF
        dist_f = (dist * dilation).astype(F32)
        outs, lses = [], []
        for hh in range(2):
            qh = jnp.where(lane < HEAD_DIM, q2, zero) if hh == 0 else \
                jnp.where(lane < HEAD_DIM, zero, q2)
            s = _dot_nt(qh, k2) + (-slopes[hh]) * dist_f
            s = jnp.where(valid, s, NEG_INF)
            m = jnp.max(s, axis=-1, keepdims=True)
            p = jnp.exp(s - m)
            den = jnp.sum(p, axis=-1, keepdims=True)
            outs.append(_dot(p.astype(BF16), v2) / den)
            lses.append(jnp.broadcast_to(m + jnp.log(den), (DIL_QB, LANES)))
        o_ref[0, pl.ds(q0, DIL_QB), :] = _half_select(lane, outs[0], outs[1]).astype(BF16)
        lse_ref[0, pl.ds(q0, DIL_QB), :] = _half_select(lane, lses[0], lses[1])
        return carry

    lax.fori_loop(0, length // DIL_QB, blk, 0)


def _dil_attention(qkv, grp, b, s):
    window, dilation = DIL_PAIRS[grp]
    assert window // 2 // dilation == DIL_HALF
    length = s // dilation
    assert length % DIL_QB == 0 and length >= DIL_KB
    width = 3 * LANES
    x3 = qkv.reshape(b, length, dilation * width)
    slopes = tuple(_alibi_slope(2 * grp + hh) for hh in range(2))
    o, lse = pl.pallas_call(
        functools.partial(_dil_kernel, length=length, dilation=dilation, slopes=slopes),
        grid=(b, dilation),
        in_specs=[pl.BlockSpec((1, length, width), lambda i, r: (i, 0, r))],
        out_specs=[pl.BlockSpec((1, length, LANES), lambda i, r: (i, 0, r)),
                   pl.BlockSpec((1, length, LANES), lambda i, r: (i, 0, r))],
        out_shape=[jax.ShapeDtypeStruct((b, length, dilation * LANES), BF16),
                   jax.ShapeDtypeStruct((b, length, dilation * LANES), F32)],
        compiler_params=_params2(),
        name=f"dil_attn_{grp}",
    )(x3)
    return o.reshape(b * s, LANES), lse.reshape(b * s, LANES)


MG_TM = 512


def _merge_kernel(x_ref, oa_ref, o0_ref, o1_ref, o2_ref, l0_ref, l1_ref, l2_ref, qm_ref, gate_ref,
                  kv_ref, wa_ref, wb_ref, wm_ref, wo_ref, g_ref, out_ref):
    lane = lax.broadcasted_iota(jnp.int32, (MG_TM, LANES), 1)
    zero = jnp.zeros((MG_TM, LANES), BF16)

    om = []
    for hp in range(MEM_HEADS // 2):
        q2 = qm_ref[:, hp * LANES:(hp + 1) * LANES]
        k2 = kv_ref[0, 0, :, hp * LANES:(hp + 1) * LANES]
        v2 = kv_ref[0, 0, :, D_M + hp * LANES:D_M + (hp + 1) * LANES]
        outs = []
        for hh in range(2):
            qh = jnp.where(lane < HEAD_DIM, q2, zero) if hh == 0 else \
                jnp.where(lane < HEAD_DIM, zero, q2)
            s = _dot_nt(qh, k2)
            m = jnp.max(s, axis=-1, keepdims=True)
            p = jnp.exp(s - m)
            den = jnp.sum(p, axis=-1, keepdims=True)
            outs.append(_dot(p.astype(BF16), v2) / den)
        om.append(_half_select(lane, outs[0], outs[1]).astype(BF16))
    o_m = jnp.concatenate(om, axis=-1)

    l0, l1, l2 = l0_ref[...], l1_ref[...], l2_ref[...]
    mx = jnp.maximum(jnp.maximum(l0, l1), l2)
    e0, e1, e2 = jnp.exp(l0 - mx), jnp.exp(l1 - mx), jnp.exp(l2 - mx)
    inv = 1.0 / (e0 + e1 + e2)
    o_b = ((e0 * inv) * o0_ref[...].astype(F32) + (e1 * inv) * o1_ref[...].astype(F32)
           + (e2 * inv) * o2_ref[...].astype(F32)).astype(BF16)

    merged = (gate_ref[:, 0:D_MODEL].astype(F32) * _dot(oa_ref[...], wa_ref[...])
              + gate_ref[:, D_MODEL:2 * D_MODEL].astype(F32) * _dot(o_b, wb_ref[...])
              + gate_ref[:, 2 * D_MODEL:3 * D_MODEL].astype(F32) * _dot(o_m, wm_ref[...]))
    y = _dot(merged.astype(BF16), wo_ref[...])
    out_ref[...] = x_ref[...] + _rms(y, g_ref[...])


def _merge(x2, oa, ob, lse, qm, gates, kv_l, wa, wb, wm, wo, g, layer, s):
    t = x2.shape[0]
    tiles_per_seq = s // MG_TM
    row = lambda i: (i, 0)
    in_specs = [
        pl.BlockSpec((MG_TM, D_MODEL), row),
        pl.BlockSpec((MG_TM, D_A), row),
        pl.BlockSpec((MG_TM, LANES), row), pl.BlockSpec((MG_TM, LANES), row),
        pl.BlockSpec((MG_TM, LANES), row),
        pl.BlockSpec((MG_TM, LANES), row), pl.BlockSpec((MG_TM, LANES), row),
        pl.BlockSpec((MG_TM, LANES), row),
        pl.BlockSpec((MG_TM, D_M), row),
        pl.BlockSpec((MG_TM, N_BRANCH * D_MODEL), row),
        pl.BlockSpec((1, 1, N_MEM, 2 * D_M), lambda i: (layer, i // tiles_per_seq, 0, 0)),
        _resident((D_A, D_MODEL)), _resident((D_B_OUT, D_MODEL)), _resident((D_M, D_MODEL)),
        _resident((D_MODEL, D_MODEL)), _resident((1, D_MODEL)),
    ]
    return pl.pallas_call(
        _merge_kernel,
        grid=(t // MG_TM,),
        in_specs=in_specs,
        out_specs=pl.BlockSpec((MG_TM, D_MODEL), row),
        out_shape=jax.ShapeDtypeStruct((t, D_MODEL), F32),
        compiler_params=_params(),
        name="merge",
    )(x2, oa, ob[0], ob[1], ob[2], lse[0], lse[1], lse[2], qm, gates, kv_l, wa, wb, wm, wo, g)


FF_TM = 512
FF_CHUNK = 256


def _ffn_kernel(xm_ref, xp_ref, xn_ref, g1_ref, wup_ref, cw_ref, cb_ref, wdn_ref, g2_ref, out_ref,
                hext_ref, f_ref, *, tiles_per_seq):
    i = pl.program_id(0)
    pos = i % tiles_per_seq
    g1 = g1_ref[...]
    x = xm_ref[...]
    hext_ref[HALO:HALO + FF_TM, :] = _rms(x, g1).astype(BF16)
    hp = jnp.where(pos == 0, 0.0, _rms(xp_ref[...], g1))
    hn = jnp.where(pos == tiles_per_seq - 1, 0.0, _rms(xn_ref[...], g1))
    hext_ref[0:HALO, :] = hp.astype(BF16)
    hext_ref[HALO + FF_TM:2 * HALO + FF_TM, :] = hn.astype(BF16)
    hext = hext_ref[...]

    def conv(u, col):
        cs = slice(col, col + FF_CHUNK)
        y = cb_ref[:, cs] + u[HALO - 1:HALO - 1 + FF_TM] * cw_ref[0:1, cs]
        y = y + u[HALO:HALO + FF_TM] * cw_ref[1:2, cs]
        return y + u[HALO + 1:HALO + 1 + FF_TM] * cw_ref[2:3, cs]

    for c in range(D_FF // FF_CHUNK):
        ca, cb = c * FF_CHUNK, D_FF + c * FF_CHUNK
        ua = conv(_dot(hext, wup_ref[:, ca:ca + FF_CHUNK]), ca)
        ub = conv(_dot(hext, wup_ref[:, cb:cb + FF_CHUNK]), cb)
        f_ref[:, ca:ca + FF_CHUNK] = (jax.nn.gelu(ua) * ub).astype(BF16)

    y = _dot(f_ref[...], wdn_ref[...])
    out_ref[...] = x + _rms(y, g2_ref[...])


def _ffn(x2, g1, w_up, cw, cb, w_dn, g2, s):
    t = x2.shape[0]
    tiles_per_seq = s // FF_TM
    per = FF_TM // HALO
    nh = t // HALO
    row = lambda i: (i, 0)
    return pl.pallas_call(
        functools.partial(_ffn_kernel, tiles_per_seq=tiles_per_seq),
        grid=(t // FF_TM,),
        in_specs=[
            pl.BlockSpec((FF_TM, D_MODEL), row),
            pl.BlockSpec((HALO, D_MODEL), lambda i: (jnp.maximum(i * per - 1, 0), 0)),
            pl.BlockSpec((HALO, D_MODEL), lambda i: (jnp.minimum((i + 1) * per, nh - 1), 0)),
            _resident((1, D_MODEL)),
            _resident((D_MODEL, 2 * D_FF)),
            _resident((3, 2 * D_FF)),
            _resident((1, 2 * D_FF)),
            _resident((D_FF, D_MODEL)),
            _resident((1, D_MODEL)),
        ],
        out_specs=pl.BlockSpec((FF_TM, D_MODEL), row),
        out_shape=jax.ShapeDtypeStruct((t, D_MODEL), F32),
        scratch_shapes=[pltpu.VMEM((FF_TM + 2 * HALO, D_MODEL), BF16),
                        pltpu.VMEM((FF_TM, D_FF), BF16)],
        compiler_params=_params(),
        name="ffn",
    )(x2, x2, x2, g1, w_up, cw, cb, w_dn, g2)


def kernel(x, mem, mem_norm_g, g_pre_mix, w_in, rpb_na, w_mem_kv, b_gate, w_br_a, w_br_b, w_br_m,
           w_out, g_post_mix, g_pre_ffn, w_up, conv_w, conv_b, w_down, g_post_ffn):
    b, s, d = x.shape
    depth = w_in.shape[0]
    assert d == D_MODEL and s % GRID_W == 0 and (b * s) % IN_TM == 0
    assert s % MG_TM == 0 and s % FF_TM == 0
    rows = s // GRID_W
    assert rows >= NA_KH and rows % NA_ROWS_PER_STEP == 0

    bf = lambda a: a.astype(BF16)
    row1 = lambda a: a.reshape(1, -1).astype(F32)

    kv = _memkv(mem, row1(mem_norm_g), bf(w_mem_kv))
    x2 = x.reshape(b * s, d)
    for l in range(depth):
        qa, ka, va, b0, b1, b2, qm, gates = _inproj(
            x2, row1(g_pre_mix[l]), bf(w_in[l]), row1(b_gate[l]))
        tab = _na_bias_table(rpb_na[l], rows)
        oa = _na_attention(qa, ka, va, tab, b, s)
        ob, lse = zip(*(_dil_attention(q, grp, b, s) for grp, q in enumerate((b0, b1, b2))))
        x2 = _merge(x2, oa, ob, lse, qm, gates, kv, bf(w_br_a[l]), bf(w_br_b[l]), bf(w_br_m[l]),
                    bf(w_out[l]), row1(g_post_mix[l]), l, s)
        x2 = _ffn(x2, row1(g_pre_ffn[l]), bf(w_up[l]), conv_w[l].astype(F32), row1(conv_b[l]),
                  bf(w_down[l]), row1(g_post_ffn[l]), s)
    return x2.reshape(b, s, d)
```

```python
import functools
import math

import jax
import jax.numpy as jnp
from jax import lax
from jax.experimental import pallas as pl
from jax.experimental.pallas import tpu as pltpu

D_MODEL = 1024
GRID_W = 64
N_MEM = 256
HEAD_DIM = 64
NA_HEADS = 6
NA_KH = 8
NA_KW = 16
DIL_PAIRS = ((128, 1), (512, 4), (2048, 16))
DIL_HEADS = 6
MEM_HEADS = 4
D_A = 384
D_B = 384
D_B_OUT = 128
D_M = 256
N_BRANCH = 3
D_IN = 3 * D_A + 3 * D_B + D_M + N_BRANCH * D_MODEL
D_FF = 2816
RMS_EPS = 1e-6
NEG_INF = -1e30

LANES = 128
HALO = 16
VMEM_LIMIT = 56 * 1024 * 1024

F32 = jnp.float32
BF16 = jnp.bfloat16


def _rms(x, g):
    return x * lax.rsqrt(jnp.mean(x * x, axis=-1, keepdims=True) + RMS_EPS) * g


def _dot(a, b):
    return jnp.dot(a, b, preferred_element_type=F32)


def _dot_nt(a, b):
    return lax.dot_general(a, b, (((1,), (1,)), ((), ())), preferred_element_type=F32)


def _resident(shape):
    n = len(shape)
    return pl.BlockSpec(shape, lambda *_: (0,) * n, pipeline_mode=pl.Buffered(1))


def _params():
    return pltpu.CompilerParams(dimension_semantics=("arbitrary",), vmem_limit_bytes=VMEM_LIMIT)


def _params2():
    return pltpu.CompilerParams(dimension_semantics=("arbitrary", "arbitrary"),
                                vmem_limit_bytes=VMEM_LIMIT)


def _memkv_kernel(mem_ref, g_ref, w_ref, o_ref):
    mn = _rms(mem_ref[0], g_ref[...]).astype(BF16)
    o_ref[0, 0] = _dot(mn, w_ref[0]).astype(BF16)


def _memkv(mem, g, w_kv):
    depth = w_kv.shape[0]
    b = mem.shape[0]
    return pl.pallas_call(
        _memkv_kernel,
        grid=(depth, b),
        in_specs=[
            pl.BlockSpec((1, N_MEM, D_MODEL), lambda l, i: (i, 0, 0)),
            pl.BlockSpec((1, D_MODEL), lambda l, i: (0, 0)),
            pl.BlockSpec((1, D_MODEL, 2 * D_M), lambda l, i: (l, 0, 0)),
        ],
        out_specs=pl.BlockSpec((1, 1, N_MEM, 2 * D_M), lambda l, i: (l, i, 0, 0)),
        out_shape=jax.ShapeDtypeStruct((depth, b, N_MEM, 2 * D_M), BF16),
        compiler_params=_params2(),
        name="memkv",
    )(mem, g, w_kv)


IN_TM = 512
IN_CHUNK = 512
_SLABS = D_IN // LANES
_A_SLABS = D_A // LANES
_M_SLABS = D_M // LANES


def _inproj_kernel(x_ref, g_ref, w_ref, bg_ref, qa_ref, ka_ref, va_ref, b0_ref, b1_ref, b2_ref,
                   qm_ref, gate_ref):
    h = _rms(x_ref[...], g_ref[...]).astype(BF16)
    scale = HEAD_DIM ** -0.5
    b_refs = (b0_ref, b1_ref, b2_ref)

    def put(slab, val):
        s = slab
        if s < 3 * _A_SLABS:
            which, j = divmod(s, _A_SLABS)
            ref = (qa_ref, ka_ref, va_ref)[which]
            if which == 0:
                val = val * scale
            ref[:, j * LANES:(j + 1) * LANES] = val.astype(BF16)
            return
        s -= 3 * _A_SLABS
        if s < 9:
            which, grp = divmod(s, 3)
            if which == 0:
                val = val * scale
            b_refs[grp][:, which * LANES:(which + 1) * LANES] = val.astype(BF16)
            return
        s -= 9
        if s < _M_SLABS:
            qm_ref[:, s * LANES:(s + 1) * LANES] = (val * scale).astype(BF16)
            return
        s -= _M_SLABS
        gate = jax.nn.sigmoid(val + bg_ref[:, s * LANES:(s + 1) * LANES])
        gate_ref[:, s * LANES:(s + 1) * LANES] = gate.astype(BF16)

    per = IN_CHUNK // LANES
    for c in range(D_IN // IN_CHUNK):
        r = _dot(h, w_ref[:, c * IN_CHUNK:(c + 1) * IN_CHUNK])
        for j in range(per):
            put(c * per + j, r[:, j * LANES:(j + 1) * LANES])


def _inproj(x2, g, w_in, b_gate):
    t = x2.shape[0]
    row = lambda i: (i, 0)
    outs = [
        jax.ShapeDtypeStruct((t, D_A), BF16), jax.ShapeDtypeStruct((t, D_A), BF16),
        jax.ShapeDtypeStruct((t, D_A), BF16),
        jax.ShapeDtypeStruct((t, 3 * LANES), BF16), jax.ShapeDtypeStruct((t, 3 * LANES), BF16),
        jax.ShapeDtypeStruct((t, 3 * LANES), BF16),
        jax.ShapeDtypeStruct((t, D_M), BF16),
        jax.ShapeDtypeStruct((t, N_BRANCH * D_MODEL), BF16),
    ]
    return pl.pallas_call(
        _inproj_kernel,
        grid=(t // IN_TM,),
        in_specs=[
            pl.BlockSpec((IN_TM, D_MODEL), row),
            _resident((1, D_MODEL)),
            _resident((D_MODEL, D_IN)),
            _resident((1, N_BRANCH * D_MODEL)),
        ],
        out_specs=[pl.BlockSpec((IN_TM, o.shape[1]), row) for o in outs],
        out_shape=outs,
        compiler_params=_params(),
        name="inproj",
    )(x2, g, w_in, b_gate)


def _pair_attention(q2, k2, v2, bias):
    return _pair_softmax_pv(_pair_scores(q2, k2), v2, bias)


def _pair_scores(q2, k2):
    lane = lax.broadcasted_iota(jnp.int32, q2.shape, 1)
    zero = jnp.zeros_like(q2)
    qs = jnp.concatenate([jnp.where(lane < HEAD_DIM, q2, zero),
                          jnp.where(lane < HEAD_DIM, zero, q2)], axis=0)
    return _dot_nt(qs, k2)


def _pair_softmax_pv(s, v2, bias):
    if bias is not None:
        s = s + bias
    m = jnp.max(s, axis=-1, keepdims=True)
    p = jnp.exp(s - m)
    den = jnp.sum(p, axis=-1, keepdims=True)
    return _dot(p.astype(BF16), v2), m, den


def _unstack_heads(a):
    m_rows = a.shape[0] // 2
    lane = lax.broadcasted_iota(jnp.int32, (m_rows, LANES), 1)
    return jnp.where(lane < HEAD_DIM, a[:m_rows], a[m_rows:])


NA_ROWS_PER_STEP = 8
NA_GROUP = 4
NA_BAND = NA_KH * GRID_W


def _na_bias_table(rpb, rows):
    kh = min(NA_KH, rows)
    heads = rpb.shape[0]
    cols = jnp.arange(GRID_W)
    c0 = jnp.clip(cols - NA_KW // 2, 0, GRID_W - NA_KW)
    cc = cols[None, :]
    valid = (cc >= c0[:, None]) & (cc < c0[:, None] + NA_KW)
    col_off = jnp.clip(cc - cols[:, None] + (NA_KW - 1), 0, 2 * NA_KW - 2)
    row_off = jnp.arange(kh)[None, :] + jnp.arange(kh)[:, None]
    tab = rpb[:, row_off][:, :, :, col_off]
    tab = jnp.where(valid[None, None, None], tab.astype(F32), NEG_INF)
    tab = tab.reshape(heads // 2, 2, kh, kh, GRID_W, GRID_W)
    return tab.transpose(0, 2, 1, 4, 3, 5).reshape(heads // 2, kh, 2 * GRID_W, kh * GRID_W)


def _na_kernel(q_ref, k_ref, v_ref, tab_ref, o_ref, *, rows):
    rb = pl.program_id(1)

    def group_body(gi, carry):
        chains = []
        for lg in range(NA_GROUP):
            lr = gi * NA_GROUP + lg
            i = rb * NA_ROWS_PER_STEP + lr
            r0 = jnp.clip(i - NA_KH // 2, 0, rows - NA_KH)
            ks = pl.multiple_of(r0 * GRID_W, GRID_W)
            qs = pl.multiple_of(lr * GRID_W, GRID_W)
            for hp in range(NA_HEADS // 2):
                chains.append((qs, ks, r0 - i + (NA_KH - 1), hp,
                               slice(hp * LANES, (hp + 1) * LANES)))
        ss = [_pair_scores(q_ref[0, pl.ds(qs, GRID_W), cs], k_ref[0, pl.ds(ks, NA_BAND), cs])
              for qs, ks, dl, hp, cs in chains]
        ps, dens = [], []
        for s, (qs, ks, dl, hp, cs) in zip(ss, chains):
            s = s + tab_ref[hp, dl]
            p = jnp.exp(s - jnp.max(s, axis=-1, keepdims=True))
            dens.append(jnp.sum(p, axis=-1, keepdims=True))
            ps.append(p.astype(BF16))
        for p, den, (qs, ks, dl, hp, cs) in zip(ps, dens, chains):
            pv = _dot(p, v_ref[0, pl.ds(ks, NA_BAND), cs])
            o_ref[0, pl.ds(qs, GRID_W), cs] = _unstack_heads(pv / den).astype(BF16)
        return carry

    lax.fori_loop(0, NA_ROWS_PER_STEP // NA_GROUP, group_body, 0)


def _na_attention(q, k, v, tab, b, s):
    rows = s // GRID_W
    qblk = NA_ROWS_PER_STEP * GRID_W
    q3, k3, v3 = (a.reshape(b, s, D_A) for a in (q, k, v))
    out = pl.pallas_call(
        functools.partial(_na_kernel, rows=rows),
        grid=(b, rows // NA_ROWS_PER_STEP),
        in_specs=[
            pl.BlockSpec((1, qblk, D_A), lambda i, j: (i, j, 0)),
            pl.BlockSpec((1, s, D_A), lambda i, j: (i, 0, 0)),
            pl.BlockSpec((1, s, D_A), lambda i, j: (i, 0, 0)),
            _resident(tab.shape),
        ],
        out_specs=pl.BlockSpec((1, qblk, D_A), lambda i, j: (i, j, 0)),
        out_shape=jax.ShapeDtypeStruct((b, s, D_A), BF16),
        compiler_params=_params2(),
        name="na_attn",
    )(q3, k3, v3, tab)
    return out.reshape(b * s, D_A)


DIL_QB = 128
DIL_HALF = 64
DIL_KB = DIL_QB + 2 * DIL_HALF
DIL_BLOCKS_PER_STEP = 8
DIL_GROUP = 8
DIL_KEY_SHIFTS = (0, -1, -2)


def _alibi_slope(head):
    return 2.0 ** (-8.0 * (head + 1) / DIL_HEADS)


def _dil_kernel(x_ref, o_ref, lse_ref, bias_ref, *, length, dilation, slopes, residues):
    nblk = length // DIL_QB

    @pl.when((pl.program_id(0) == 0) & (pl.program_id(1) == 0))
    def _():
        rel0 = (lax.broadcasted_iota(jnp.int32, (DIL_QB, DIL_KB), 1)
                - lax.broadcasted_iota(jnp.int32, (DIL_QB, DIL_KB), 0))
        for c, shift in enumerate(DIL_KEY_SHIFTS):
            dist = jnp.abs(rel0 + shift * DIL_HALF)
            dist_f = (dist * dilation).astype(F32)
            for hh in range(2):
                bias = jnp.where(dist <= DIL_HALF, (-slopes[hh]) * dist_f, NEG_INF)
                bias_ref[c, hh * DIL_QB:(hh + 1) * DIL_QB, :] = bias

    def group(chains):
        geo = []
        for r, bi in chains:
            q0 = bi * DIL_QB
            if isinstance(bi, int):
                ks = min(max(q0 - DIL_HALF, 0), length - DIL_KB)
            else:
                q0 = pl.multiple_of(q0, DIL_QB)
                ks = pl.multiple_of(jnp.clip(q0 - DIL_HALF, 0, length - DIL_KB), DIL_HALF)
            geo.append((r, r * 3 * LANES, q0, ks, (q0 - ks) // DIL_HALF))
        ss = [_pair_scores(x_ref[0, pl.ds(q0, DIL_QB), base:base + LANES],
                           x_ref[0, pl.ds(ks, DIL_KB), base + LANES:base + 2 * LANES])
              for r, base, q0, ks, case in geo]
        ps, ms, dens = [], [], []
        for s, (r, base, q0, ks, case) in zip(ss, geo):
            s = s + bias_ref[case]
            m = jnp.max(s, axis=-1, keepdims=True)
            p = jnp.exp(s - m)
            ms.append(m)
            dens.append(jnp.sum(p, axis=-1, keepdims=True))
            ps.append(p.astype(BF16))
        for p, m, den, (r, base, q0, ks, case) in zip(ps, ms, dens, geo):
            pv = _dot(p, x_ref[0, pl.ds(ks, DIL_KB), base + 2 * LANES:base + 3 * LANES])
            o_ref[0, pl.ds(q0, DIL_QB), r * LANES:(r + 1) * LANES] = \
                _unstack_heads(pv / den).astype(BF16)
            lse = jnp.broadcast_to(m + jnp.log(den), (2 * DIL_QB, LANES))
            lse_ref[0, pl.ds(q0, DIL_QB), r * LANES:(r + 1) * LANES] = _unstack_heads(lse)

    if residues * nblk <= DIL_GROUP:
        group([(r, bi) for r in range(residues) for bi in range(nblk)])
    else:
        assert nblk % DIL_GROUP == 0
        for r in range(residues):
            def body(gi, carry, r=r):
                group([(r, gi * DIL_GROUP + j) for j in range(DIL_GROUP)])
                return carry
            lax.fori_loop(0, nblk // DIL_GROUP, body, 0)


def _dil_attention(qkv, grp, b, s):
    window, dilation = DIL_PAIRS[grp]
    assert window // 2 // dilation == DIL_HALF
    length = s // dilation
    assert length % DIL_QB == 0 and length >= DIL_KB
    nblk = length // DIL_QB
    residues = min(dilation, max(1, DIL_BLOCKS_PER_STEP // nblk))
    assert dilation % residues == 0
    width = 3 * LANES
    x3 = qkv.reshape(b, length, dilation * width)
    slopes = tuple(_alibi_slope(2 * grp + hh) for hh in range(2))
    o, lse = pl.pallas_call(
        functools.partial(_dil_kernel, length=length, dilation=dilation, slopes=slopes,
                          residues=residues),
        grid=(b, dilation // residues),
        in_specs=[pl.BlockSpec((1, length, residues * width), lambda i, r: (i, 0, r))],
        out_specs=[pl.BlockSpec((1, length, residues * LANES), lambda i, r: (i, 0, r)),
                   pl.BlockSpec((1, length, residues * LANES), lambda i, r: (i, 0, r))],
        out_shape=[jax.ShapeDtypeStruct((b, length, dilation * LANES), BF16),
                   jax.ShapeDtypeStruct((b, length, dilation * LANES), F32)],
        scratch_shapes=[pltpu.VMEM((len(DIL_KEY_SHIFTS), 2 * DIL_QB, DIL_KB), F32)],
        compiler_params=_params2(),
        name=f"dil_attn_{grp}",
    )(x3)
    return o.reshape(b * s, LANES), lse.reshape(b * s, LANES)


MG_TM = 512


def _merge_kernel(x_ref, oa_ref, o0_ref, o1_ref, o2_ref, l0_ref, l1_ref, l2_ref, qm_ref, gate_ref,
                  kv_ref, wa_ref, wb_ref, wm_ref, wo_ref, g_ref, out_ref):
    om = []
    for hp in range(MEM_HEADS // 2):
        pv, _, den = _pair_attention(qm_ref[:, hp * LANES:(hp + 1) * LANES],
                                     kv_ref[0, 0, :, hp * LANES:(hp + 1) * LANES],
                                     kv_ref[0, 0, :, D_M + hp * LANES:D_M + (hp + 1) * LANES], None)
        om.append(_unstack_heads(pv / den).astype(BF16))
    o_m = jnp.concatenate(om, axis=-1)

    l0, l1, l2 = l0_ref[...], l1_ref[...], l2_ref[...]
    mx = jnp.maximum(jnp.maximum(l0, l1), l2)
    e0, e1, e2 = jnp.exp(l0 - mx), jnp.exp(l1 - mx), jnp.exp(l2 - mx)
    inv = 1.0 / (e0 + e1 + e2)
    o_b = ((e0 * inv) * o0_ref[...].astype(F32) + (e1 * inv) * o1_ref[...].astype(F32)
           + (e2 * inv) * o2_ref[...].astype(F32)).astype(BF16)

    merged = (gate_ref[:, 0:D_MODEL].astype(F32) * _dot(oa_ref[...], wa_ref[...])
              + gate_ref[:, D_MODEL:2 * D_MODEL].astype(F32) * _dot(o_b, wb_ref[...])
              + gate_ref[:, 2 * D_MODEL:3 * D_MODEL].astype(F32) * _dot(o_m, wm_ref[...]))
    y = _dot(merged.astype(BF16), wo_ref[...])
    out_ref[...] = x_ref[...] + _rms(y, g_ref[...])


def _merge(x2, oa, ob, lse, qm, gates, kv_l, wa, wb, wm, wo, g, layer, s):
    t = x2.shape[0]
    tiles_per_seq = s // MG_TM
    row = lambda i: (i, 0)
    in_specs = [
        pl.BlockSpec((MG_TM, D_MODEL), row),
        pl.BlockSpec((MG_TM, D_A), row),
        pl.BlockSpec((MG_TM, LANES), row), pl.BlockSpec((MG_TM, LANES), row),
        pl.BlockSpec((MG_TM, LANES), row),
        pl.BlockSpec((MG_TM, LANES), row), pl.BlockSpec((MG_TM, LANES), row),
        pl.BlockSpec((MG_TM, LANES), row),
        pl.BlockSpec((MG_TM, D_M), row),
        pl.BlockSpec((MG_TM, N_BRANCH * D_MODEL), row),
        pl.BlockSpec((1, 1, N_MEM, 2 * D_M), lambda i: (layer, i // tiles_per_seq, 0, 0)),
        _resident((D_A, D_MODEL)), _resident((D_B_OUT, D_MODEL)), _resident((D_M, D_MODEL)),
        _resident((D_MODEL, D_MODEL)), _resident((1, D_MODEL)),
    ]
    return pl.pallas_call(
        _merge_kernel,
        grid=(t // MG_TM,),
        in_specs=in_specs,
        out_specs=pl.BlockSpec((MG_TM, D_MODEL), row),
        out_shape=jax.ShapeDtypeStruct((t, D_MODEL), F32),
        compiler_params=_params(),
        name="merge",
    )(x2, oa, ob[0], ob[1], ob[2], lse[0], lse[1], lse[2], qm, gates, kv_l, wa, wb, wm, wo, g)


FF_TM = 512
FF_CHUNK = 256


def _ffn_kernel(xm_ref, xp_ref, xn_ref, g1_ref, wup_ref, cw_ref, cb_ref, wdn_ref, g2_ref, out_ref,
                hext_ref, f_ref, *, tiles_per_seq):
    i = pl.program_id(0)
    pos = i % tiles_per_seq
    g1 = g1_ref[...]
    x = xm_ref[...]
    hext_ref[HALO:HALO + FF_TM, :] = _rms(x, g1).astype(BF16)
    hp = jnp.where(pos == 0, 0.0, _rms(xp_ref[...], g1))
    hn = jnp.where(pos == tiles_per_seq - 1, 0.0, _rms(xn_ref[...], g1))
    hext_ref[0:HALO, :] = hp.astype(BF16)
    hext_ref[HALO + FF_TM:2 * HALO + FF_TM, :] = hn.astype(BF16)
    hext = hext_ref[...]

    def conv(u, col):
        cs = slice(col, col + FF_CHUNK)
        y = cb_ref[:, cs] + u[HALO - 1:HALO - 1 + FF_TM] * cw_ref[0:1, cs]
        y = y + u[HALO:HALO + FF_TM] * cw_ref[1:2, cs]
        return y + u[HALO + 1:HALO + 1 + FF_TM] * cw_ref[2:3, cs]

    for c in range(D_FF // FF_CHUNK):
        ca, cb = c * FF_CHUNK, D_FF + c * FF_CHUNK
        ua = conv(_dot(hext, wup_ref[:, ca:ca + FF_CHUNK]), ca)
        ub = conv(_dot(hext, wup_ref[:, cb:cb + FF_CHUNK]), cb)
        f_ref[:, ca:ca + FF_CHUNK] = (jax.nn.gelu(ua) * ub).astype(BF16)

    y = _dot(f_ref[...], wdn_ref[...])
    out_ref[...] = x + _rms(y, g2_ref[...])


def _ffn(x2, g1, w_up, cw, cb, w_dn, g2, s):
    t = x2.shape[0]
    tiles_per_seq = s // FF_TM
    per = FF_TM // HALO
    nh = t // HALO
    row = lambda i: (i, 0)
    return pl.pallas_call(
        functools.partial(_ffn_kernel, tiles_per_seq=tiles_per_seq),
        grid=(t // FF_TM,),
        in_specs=[
            pl.BlockSpec((FF_TM, D_MODEL), row),
            pl.BlockSpec((HALO, D_MODEL), lambda i: (jnp.maximum(i * per - 1, 0), 0)),
            pl.BlockSpec((HALO, D_MODEL), lambda i: (jnp.minimum((i + 1) * per, nh - 1), 0)),
            _resident((1, D_MODEL)),
            _resident((D_MODEL, 2 * D_FF)),
            _resident((3, 2 * D_FF)),
            _resident((1, 2 * D_FF)),
            _resident((D_FF, D_MODEL)),
            _resident((1, D_MODEL)),
        ],
        out_specs=pl.BlockSpec((FF_TM, D_MODEL), row),
        out_shape=jax.ShapeDtypeStruct((t, D_MODEL), F32),
        scratch_shapes=[pltpu.VMEM((FF_TM + 2 * HALO, D_MODEL), BF16),
                        pltpu.VMEM((FF_TM, D_FF), BF16)],
        compiler_params=_params(),
        name="ffn",
    )(x2, x2, x2, g1, w_up, cw, cb, w_dn, g2)


def kernel(x, mem, mem_norm_g, g_pre_mix, w_in, rpb_na, w_mem_kv, b_gate, w_br_a, w_br_b, w_br_m,
           w_out, g_post_mix, g_pre_ffn, w_up, conv_w, conv_b, w_down, g_post_ffn):
    b, s, d = x.shape
    depth = w_in.shape[0]
    assert d == D_MODEL and s % GRID_W == 0 and (b * s) % IN_TM == 0
    assert s % MG_TM == 0 and s % FF_TM == 0
    rows = s // GRID_W
    assert rows >= NA_KH and rows % NA_ROWS_PER_STEP == 0

    bf = lambda a: a.astype(BF16)
    row1 = lambda a: a.reshape(1, -1).astype(F32)

    kv = _memkv(mem, row1(mem_norm_g), bf(w_mem_kv))
    x2 = x.reshape(b * s, d)
    for l in range(depth):
        qa, ka, va, b0, b1, b2, qm, gates = _inproj(
            x2, row1(g_pre_mix[l]), bf(w_in[l]), row1(b_gate[l]))
        tab = _na_bias_table(rpb_na[l], rows)
        oa = _na_attention(qa, ka, va, tab, b, s)
        ob, lse = zip(*(_dil_attention(q, grp, b, s) for grp, q in enumerate((b0, b1, b2))))
        x2 = _merge(x2, oa, ob, lse, qm, gates, kv, bf(w_br_a[l]), bf(w_br_b[l]), bf(w_br_m[l]),
                    bf(w_out[l]), row1(g_post_mix[l]), l, s)
        x2 = _ffn(x2, row1(g_pre_ffn[l]), bf(w_up[l]), conv_w[l].astype(F32), row1(conv_b[l]),
                  bf(w_down[l]), row1(g_post_ffn[l]), s)
    return x2.reshape(b, s, d)
```

```python
import functools

import jax
import jax.numpy as jnp
from jax import lax
from jax.experimental import pallas as pl
from jax.experimental.pallas import tpu as pltpu

D_MODEL = 1024
GRID_W = 64
N_MEM = 256
HEAD_DIM = 64
NA_HEADS = 6
NA_KH = 8
NA_KW = 16
DIL_PAIRS = ((128, 1), (512, 4), (2048, 16))
DIL_HEADS = 6
MEM_HEADS = 4
D_A = 384
D_B = 384
D_B_OUT = 128
D_M = 256
N_BRANCH = 3
D_IN = 3 * D_A + 3 * D_B + D_M + N_BRANCH * D_MODEL
D_FF = 2816
RMS_EPS = 1e-6
NEG_INF = -1e30

LANES = 128
HALO = 16
VMEM_LIMIT = 56 * 1024 * 1024
ROW_TILE = 512

F32 = jnp.float32
BF16 = jnp.bfloat16


def _rms(x, g):
    return x * lax.rsqrt(jnp.mean(x * x, axis=-1, keepdims=True) + RMS_EPS) * g


def _dot(a, b):
    return jnp.dot(a, b, preferred_element_type=F32)


def _dot_nt(a, b):
    return lax.dot_general(a, b, (((1,), (1,)), ((), ())), preferred_element_type=F32)


def _layer_param(shape, layer):
    zeros = (0,) * len(shape)
    return pl.BlockSpec((1,) + tuple(shape), lambda *_: (layer,) + zeros,
                        pipeline_mode=pl.Buffered(1))


def _params(n_axes=1):
    return pltpu.CompilerParams(dimension_semantics=("arbitrary",) * n_axes,
                                vmem_limit_bytes=VMEM_LIMIT)


def _memkv_kernel(mem_ref, g_ref, w_ref, o_ref):
    mn = _rms(mem_ref[0], g_ref[...]).astype(BF16)
    o_ref[0, 0] = _dot(mn, w_ref[0]).astype(BF16)


def _memkv(mem, g, w_kv):
    depth = w_kv.shape[0]
    b = mem.shape[0]
    return pl.pallas_call(
        _memkv_kernel,
        grid=(depth, b),
        in_specs=[
            pl.BlockSpec((1, N_MEM, D_MODEL), lambda l, i: (i, 0, 0)),
            pl.BlockSpec((1, D_MODEL), lambda l, i: (0, 0)),
            pl.BlockSpec((1, D_MODEL, 2 * D_M), lambda l, i: (l, 0, 0)),
        ],
        out_specs=pl.BlockSpec((1, 1, N_MEM, 2 * D_M), lambda l, i: (l, i, 0, 0)),
        out_shape=jax.ShapeDtypeStruct((depth, b, N_MEM, 2 * D_M), BF16),
        compiler_params=_params(2),
        name="memkv",
    )(mem, g, w_kv)


IN_CHUNK = 512
_A_SLABS = D_A // LANES
_M_SLABS = D_M // LANES
_N_GROUPS = len(DIL_PAIRS)


def _inproj_kernel(x_ref, g_ref, w_ref, bg_ref, qa_ref, ka_ref, va_ref, b0_ref, b1_ref, b2_ref,
                   qm_ref, gate_ref, stage_ref):
    h = _rms(x_ref[...], g_ref[0]).astype(BF16)
    scale = HEAD_DIM ** -0.5
    b_refs = (b0_ref, b1_ref, b2_ref)

    def put_dilated(which, grp, val):
        d = DIL_PAIRS[grp][1]
        cs = slice(which * LANES, (which + 1) * LANES)
        if d == 1:
            b_refs[grp][0, 0, :, cs] = val.astype(BF16)
            return
        stage = stage_ref.at[(grp - 1) * 3 + which]
        stage[...] = val
        for r in range(d):
            b_refs[grp][0, r, :, cs] = stage[pl.ds(r, ROW_TILE // d, stride=d), :].astype(BF16)

    def put(slab, val):
        s = slab
        if s < 3 * _A_SLABS:
            which, j = divmod(s, _A_SLABS)
            ref = (qa_ref, ka_ref, va_ref)[which]
            if which == 0:
                val = val * scale
            ref[:, j * LANES:(j + 1) * LANES] = val.astype(BF16)
            return
        s -= 3 * _A_SLABS
        if s < 3 * _N_GROUPS:
            which, grp = divmod(s, _N_GROUPS)
            put_dilated(which, grp, val * scale if which == 0 else val)
            return
        s -= 3 * _N_GROUPS
        if s < _M_SLABS:
            qm_ref[:, s * LANES:(s + 1) * LANES] = (val * scale).astype(BF16)
            return
        s -= _M_SLABS
        gate = jax.nn.sigmoid(val + bg_ref[0, :, s * LANES:(s + 1) * LANES])
        gate_ref[:, s * LANES:(s + 1) * LANES] = gate.astype(BF16)

    per = IN_CHUNK // LANES
    for c in range(D_IN // IN_CHUNK):
        r = _dot(h, w_ref[0, :, c * IN_CHUNK:(c + 1) * IN_CHUNK])
        for j in range(per):
            put(c * per + j, r[:, j * LANES:(j + 1) * LANES])


def _inproj(x2, g, w_in, b_gate, layer, b, s):
    t = x2.shape[0]
    tiles_per_seq = s // ROW_TILE
    row = lambda i: (i, 0)
    outs = [jax.ShapeDtypeStruct((t, D_A), BF16)] * 3
    out_specs = [pl.BlockSpec((ROW_TILE, D_A), row)] * 3
    for _, d in DIL_PAIRS:
        assert ROW_TILE % (d * HALO) == 0
        outs.append(jax.ShapeDtypeStruct((b, d, s // d, 3 * LANES), BF16))
        out_specs.append(pl.BlockSpec((1, d, ROW_TILE // d, 3 * LANES),
                                      lambda i: (i // tiles_per_seq, 0, i % tiles_per_seq, 0)))
    outs += [jax.ShapeDtypeStruct((t, D_M), BF16),
             jax.ShapeDtypeStruct((t, N_BRANCH * D_MODEL), BF16)]
    out_specs += [pl.BlockSpec((ROW_TILE, D_M), row),
                  pl.BlockSpec((ROW_TILE, N_BRANCH * D_MODEL), row)]
    return pl.pallas_call(
        _inproj_kernel,
        grid=(t // ROW_TILE,),
        in_specs=[
            pl.BlockSpec((ROW_TILE, D_MODEL), row),
            _layer_param((1, D_MODEL), layer),
            _layer_param((D_MODEL, D_IN), layer),
            _layer_param((1, N_BRANCH * D_MODEL), layer),
        ],
        out_specs=out_specs,
        out_shape=outs,
        scratch_shapes=[pltpu.VMEM((3 * (_N_GROUPS - 1), ROW_TILE, LANES), F32)],
        compiler_params=_params(),
        name="inproj",
    )(x2, g, w_in, b_gate)


def _pair_scores(q2, k2):
    lane = lax.broadcasted_iota(jnp.int32, q2.shape, 1)
    zero = jnp.zeros_like(q2)
    qs = jnp.concatenate([jnp.where(lane < HEAD_DIM, q2, zero),
                          jnp.where(lane < HEAD_DIM, zero, q2)], axis=0)
    return _dot_nt(qs, k2)


def _softmax_parts(s):
    m = jnp.max(s, axis=-1, keepdims=True)
    p = jnp.exp(s - m)
    return m, p.astype(BF16), jnp.sum(p, axis=-1, keepdims=True)


def _unstack_heads(a):
    m_rows = a.shape[0] // 2
    lane = lax.broadcasted_iota(jnp.int32, (m_rows, LANES), 1)
    return jnp.where(lane < HEAD_DIM, a[:m_rows], a[m_rows:])


NA_ROWS_PER_STEP = 8
NA_GROUP = 4
NA_BAND = NA_KH * GRID_W


def _na_bias_table(rpb, rows):
    kh = min(NA_KH, rows)
    depth, heads = rpb.shape[:2]
    pad = GRID_W - NA_KW
    rp = jnp.pad(rpb.astype(F32), ((0, 0), (0, 0), (0, 0), (pad, pad)))
    toe = jnp.stack([rp[..., GRID_W - 1 - q:2 * GRID_W - 1 - q] for q in range(GRID_W)], axis=3)
    cols = jnp.arange(GRID_W)
    c0 = jnp.clip(cols - NA_KW // 2, 0, GRID_W - NA_KW)
    valid = (cols[None, :] >= c0[:, None]) & (cols[None, :] < c0[:, None] + NA_KW)
    toe = jnp.where(valid, toe, NEG_INF)
    tab = jnp.stack([toe[:, :, dl:dl + kh] for dl in range(kh)], axis=2)
    tab = tab.reshape(depth, heads // 2, 2, kh, kh, GRID_W, GRID_W)
    return tab.transpose(0, 1, 3, 2, 5, 4, 6).reshape(depth, heads // 2, kh, 2 * GRID_W,
                                                     kh * GRID_W)


def _na_kernel(q_ref, k_ref, v_ref, tab_ref, o_ref, *, rows):
    rb = pl.program_id(1)

    def group_body(gi, carry):
        chains = []
        for lg in range(NA_GROUP):
            lr = gi * NA_GROUP + lg
            i = rb * NA_ROWS_PER_STEP + lr
            r0 = jnp.clip(i - NA_KH // 2, 0, rows - NA_KH)
            ks = pl.multiple_of(r0 * GRID_W, GRID_W)
            qs = pl.multiple_of(lr * GRID_W, GRID_W)
            for hp in range(NA_HEADS // 2):
                chains.append((qs, ks, r0 - i + (NA_KH - 1), hp,
                               slice(hp * LANES, (hp + 1) * LANES)))
        ss = [_pair_scores(q_ref[0, pl.ds(qs, GRID_W), cs], k_ref[0, pl.ds(ks, NA_BAND), cs])
              for qs, ks, dl, hp, cs in chains]
        parts = [_softmax_parts(s + tab_ref[0, hp, dl])
                 for s, (qs, ks, dl, hp, cs) in zip(ss, chains)]
        for (_, p, den), (qs, ks, dl, hp, cs) in zip(parts, chains):
            pv = _dot(p, v_ref[0, pl.ds(ks, NA_BAND), cs])
            o_ref[0, pl.ds(qs, GRID_W), cs] = _unstack_heads(pv / den).astype(BF16)
        return carry

    lax.fori_loop(0, NA_ROWS_PER_STEP // NA_GROUP, group_body, 0)


def _na_attention(q, k, v, tab, layer, b, s):
    rows = s // GRID_W
    qblk = NA_ROWS_PER_STEP * GRID_W
    q3, k3, v3 = (a.reshape(b, s, D_A) for a in (q, k, v))
    out = pl.pallas_call(
        functools.partial(_na_kernel, rows=rows),
        grid=(b, rows // NA_ROWS_PER_STEP),
        in_specs=[
            pl.BlockSpec((1, qblk, D_A), lambda i, j: (i, j, 0)),
            pl.BlockSpec((1, s, D_A), lambda i, j: (i, 0, 0)),
            pl.BlockSpec((1, s, D_A), lambda i, j: (i, 0, 0)),
            _layer_param(tab.shape[1:], layer),
        ],
        out_specs=pl.BlockSpec((1, qblk, D_A), lambda i, j: (i, j, 0)),
        out_shape=jax.ShapeDtypeStruct((b, s, D_A), BF16),
        compiler_params=_params(2),
        name="na_attn",
    )(q3, k3, v3, tab)
    return out.reshape(b * s, D_A)


DIL_QB = 128
DIL_HALF = 64
DIL_KB = DIL_QB + 2 * DIL_HALF
DIL_GROUP = 8
DIL_KEY_SHIFTS = (0, -1, -2)


def _alibi_slope(head):
    return 2.0 ** (-8.0 * (head + 1) / DIL_HEADS)


def _dil_kernel(x_ref, o_ref, lse_ref, bias_ref, *, length, dilation, slopes, residues):
    nblk = length // DIL_QB

    @pl.when((pl.program_id(0) == 0) & (pl.program_id(1) == 0))
    def _():
        rel0 = (lax.broadcasted_iota(jnp.int32, (DIL_QB, DIL_KB), 1)
                - lax.broadcasted_iota(jnp.int32, (DIL_QB, DIL_KB), 0))
        for c, shift in enumerate(DIL_KEY_SHIFTS):
            dist = jnp.abs(rel0 + shift * DIL_HALF)
            dist_f = (dist * dilation).astype(F32)
            for hh in range(2):
                bias = jnp.where(dist <= DIL_HALF, (-slopes[hh]) * dist_f, NEG_INF)
                bias_ref[c, hh * DIL_QB:(hh + 1) * DIL_QB, :] = bias

    def group(chains):
        geo = []
        for r, bi in chains:
            q0 = bi * DIL_QB
            if isinstance(bi, int):
                ks = min(max(q0 - DIL_HALF, 0), length - DIL_KB)
            else:
                q0 = pl.multiple_of(q0, DIL_QB)
                ks = pl.multiple_of(jnp.clip(q0 - DIL_HALF, 0, length - DIL_KB), DIL_HALF)
            geo.append((r, q0, ks, (q0 - ks) // DIL_HALF))
        ss = [_pair_scores(x_ref[0, r, pl.ds(q0, DIL_QB), 0:LANES],
                           x_ref[0, r, pl.ds(ks, DIL_KB), LANES:2 * LANES])
              for r, q0, ks, case in geo]
        parts = [_softmax_parts(s + bias_ref[case]) for s, (r, q0, ks, case) in zip(ss, geo)]
        for (m, p, den), (r, q0, ks, case) in zip(parts, geo):
            pv = _dot(p, x_ref[0, r, pl.ds(ks, DIL_KB), 2 * LANES:3 * LANES])
            o_ref[0, r, pl.ds(q0, DIL_QB), :] = _unstack_heads(pv / den).astype(BF16)
            lse = jnp.broadcast_to(m + jnp.log(den), (2 * DIL_QB, LANES))
            lse_ref[0, r, pl.ds(q0, DIL_QB), :] = _unstack_heads(lse)

    if residues * nblk <= DIL_GROUP:
        group([(r, bi) for r in range(residues) for bi in range(nblk)])
    else:
        assert nblk % DIL_GROUP == 0
        for r in range(residues):
            def body(gi, carry, r=r):
                group([(r, gi * DIL_GROUP + j) for j in range(DIL_GROUP)])
                return carry
            lax.fori_loop(0, nblk // DIL_GROUP, body, 0)


def _dil_attention(qkv, grp):
    window, dilation = DIL_PAIRS[grp]
    assert window // 2 // dilation == DIL_HALF
    b, _, length, width = qkv.shape
    assert length % DIL_QB == 0 and length >= DIL_KB
    nblk = length // DIL_QB
    residues = min(dilation, max(1, DIL_GROUP // nblk))
    assert dilation % residues == 0
    slopes = tuple(_alibi_slope(2 * grp + hh) for hh in range(2))
    blk = lambda w: pl.BlockSpec((1, residues, length, w), lambda i, r: (i, r, 0, 0))
    return pl.pallas_call(
        functools.partial(_dil_kernel, length=length, dilation=dilation, slopes=slopes,
                          residues=residues),
        grid=(b, dilation // residues),
        in_specs=[blk(width)],
        out_specs=[blk(LANES), blk(LANES)],
        out_shape=[jax.ShapeDtypeStruct((b, dilation, length, LANES), BF16),
                   jax.ShapeDtypeStruct((b, dilation, length, LANES), F32)],
        scratch_shapes=[pltpu.VMEM((len(DIL_KEY_SHIFTS), 2 * DIL_QB, DIL_KB), F32)],
        compiler_params=_params(2),
        name=f"dil_attn_{grp}",
    )(qkv)


def _merge_kernel(x_ref, oa_ref, o0_ref, o1_ref, o2_ref, l0_ref, l1_ref, l2_ref, qm_ref, gate_ref,
                  kv_ref, wa_ref, wb_ref, wm_ref, wo_ref, g_ref, out_ref, stage_ref):
    om = []
    for hp in range(MEM_HEADS // 2):
        s = _pair_scores(qm_ref[:, hp * LANES:(hp + 1) * LANES],
                         kv_ref[0, 0, :, hp * LANES:(hp + 1) * LANES])
        _, p, den = _softmax_parts(s)
        pv = _dot(p, kv_ref[0, 0, :, D_M + hp * LANES:D_M + (hp + 1) * LANES])
        om.append(_unstack_heads(pv / den).astype(BF16))
    o_m = jnp.concatenate(om, axis=-1)

    def natural_order(ref, grp, slot):
        d = DIL_PAIRS[grp][1]
        if d == 1:
            return ref[0, 0].astype(F32)
        stage = stage_ref.at[slot]
        for r in range(d):
            stage[pl.ds(r, ROW_TILE // d, stride=d), :] = ref[0, r].astype(F32)
        return stage[...]

    o_refs, l_refs = (o0_ref, o1_ref, o2_ref), (l0_ref, l1_ref, l2_ref)
    os_ = [natural_order(o_refs[g], g, 2 * (g - 1)) for g in range(_N_GROUPS)]
    ls = [natural_order(l_refs[g], g, 2 * (g - 1) + 1) for g in range(_N_GROUPS)]
    mx = jnp.maximum(jnp.maximum(ls[0], ls[1]), ls[2])
    es = [jnp.exp(l - mx) for l in ls]
    inv = 1.0 / (es[0] + es[1] + es[2])
    o_b = ((es[0] * inv) * os_[0] + (es[1] * inv) * os_[1] + (es[2] * inv) * os_[2]).astype(BF16)

    merged = (gate_ref[:, 0:D_MODEL].astype(F32) * _dot(oa_ref[...], wa_ref[0])
              + gate_ref[:, D_MODEL:2 * D_MODEL].astype(F32) * _dot(o_b, wb_ref[0])
              + gate_ref[:, 2 * D_MODEL:3 * D_MODEL].astype(F32) * _dot(o_m, wm_ref[0]))
    y = _dot(merged.astype(BF16), wo_ref[0])
    out_ref[...] = x_ref[...] + _rms(y, g_ref[0])


def _merge(x2, oa, ob, lse, qm, gates, kv, wa, wb, wm, wo, g, layer, s):
    t = x2.shape[0]
    tiles_per_seq = s // ROW_TILE
    row = lambda i: (i, 0)
    dil = [pl.BlockSpec((1, d, ROW_TILE // d, LANES),
                        lambda i: (i // tiles_per_seq, 0, i % tiles_per_seq, 0))
           for _, d in DIL_PAIRS]
    in_specs = [
        pl.BlockSpec((ROW_TILE, D_MODEL), row),
        pl.BlockSpec((ROW_TILE, D_A), row),
        *dil, *dil,
        pl.BlockSpec((ROW_TILE, D_M), row),
        pl.BlockSpec((ROW_TILE, N_BRANCH * D_MODEL), row),
        pl.BlockSpec((1, 1, N_MEM, 2 * D_M), lambda i: (layer, i // tiles_per_seq, 0, 0)),
        _layer_param((D_A, D_MODEL), layer), _layer_param((D_B_OUT, D_MODEL), layer),
        _layer_param((D_M, D_MODEL), layer), _layer_param((D_MODEL, D_MODEL), layer),
        _layer_param((1, D_MODEL), layer),
    ]
    return pl.pallas_call(
        _merge_kernel,
        grid=(t // ROW_TILE,),
        in_specs=in_specs,
        out_specs=pl.BlockSpec((ROW_TILE, D_MODEL), row),
        out_shape=jax.ShapeDtypeStruct((t, D_MODEL), F32),
        scratch_shapes=[pltpu.VMEM((2 * (_N_GROUPS - 1), ROW_TILE, LANES), F32)],
        compiler_params=_params(),
        name="merge",
    )(x2, oa, *ob, *lse, qm, gates, kv, wa, wb, wm, wo, g)


FF_CHUNK = 256


def _ffn_kernel(xm_ref, xp_ref, xn_ref, g1_ref, wup_ref, cw_ref, cb_ref, wdn_ref, g2_ref, out_ref,
                hext_ref, f_ref, *, tiles_per_seq):
    i = pl.program_id(0)
    pos = i % tiles_per_seq
    g1 = g1_ref[0]
    x = xm_ref[...]
    hext_ref[HALO:HALO + ROW_TILE, :] = _rms(x, g1).astype(BF16)
    hp = jnp.where(pos == 0, 0.0, _rms(xp_ref[...], g1))
    hn = jnp.where(pos == tiles_per_seq - 1, 0.0, _rms(xn_ref[...], g1))
    hext_ref[0:HALO, :] = hp.astype(BF16)
    hext_ref[HALO + ROW_TILE:2 * HALO + ROW_TILE, :] = hn.astype(BF16)
    hext = hext_ref[...]

    def conv(u, col):
        cs = slice(col, col + FF_CHUNK)
        y = cb_ref[0, :, cs] + u[HALO - 1:HALO - 1 + ROW_TILE] * cw_ref[0, 0:1, cs]
        y = y + u[HALO:HALO + ROW_TILE] * cw_ref[0, 1:2, cs]
        return y + u[HALO + 1:HALO + 1 + ROW_TILE] * cw_ref[0, 2:3, cs]

    for c in range(D_FF // FF_CHUNK):
        ca, cb = c * FF_CHUNK, D_FF + c * FF_CHUNK
        ua = conv(_dot(hext, wup_ref[0, :, ca:ca + FF_CHUNK]), ca)
        ub = conv(_dot(hext, wup_ref[0, :, cb:cb + FF_CHUNK]), cb)
        f_ref[:, ca:ca + FF_CHUNK] = (jax.nn.gelu(ua) * ub).astype(BF16)

    y = _dot(f_ref[...], wdn_ref[0])
    out_ref[...] = x + _rms(y, g2_ref[0])


def _ffn(x2, g1, w_up, cw, cb, w_dn, g2, layer, s):
    t = x2.shape[0]
    tiles_per_seq = s // ROW_TILE
    per = ROW_TILE // HALO
    nh = t // HALO
    row = lambda i: (i, 0)
    return pl.pallas_call(
        functools.partial(_ffn_kernel, tiles_per_seq=tiles_per_seq),
        grid=(t // ROW_TILE,),
        in_specs=[
            pl.BlockSpec((ROW_TILE, D_MODEL), row),
            pl.BlockSpec((HALO, D_MODEL), lambda i: (jnp.maximum(i * per - 1, 0), 0)),
            pl.BlockSpec((HALO, D_MODEL), lambda i: (jnp.minimum((i + 1) * per, nh - 1), 0)),
            _layer_param((1, D_MODEL), layer),
            _layer_param((D_MODEL, 2 * D_FF), layer),
            _layer_param((3, 2 * D_FF), layer),
            _layer_param((1, 2 * D_FF), layer),
            _layer_param((D_FF, D_MODEL), layer),
            _layer_param((1, D_MODEL), layer),
        ],
        out_specs=pl.BlockSpec((ROW_TILE, D_MODEL), row),
        out_shape=jax.ShapeDtypeStruct((t, D_MODEL), F32),
        scratch_shapes=[pltpu.VMEM((ROW_TILE + 2 * HALO, D_MODEL), BF16),
                        pltpu.VMEM((ROW_TILE, D_FF), BF16)],
        compiler_params=_params(),
        name="ffn",
    )(x2, x2, x2, g1, w_up, cw, cb, w_dn, g2)


def kernel(x, mem, mem_norm_g, g_pre_mix, w_in, rpb_na, w_mem_kv, b_gate, w_br_a, w_br_b, w_br_m,
           w_out, g_post_mix, g_pre_ffn, w_up, conv_w, conv_b, w_down, g_post_ffn):
    b, s, d = x.shape
    depth = w_in.shape[0]
    assert d == D_MODEL and s % GRID_W == 0 and s % ROW_TILE == 0
    rows = s // GRID_W
    assert rows >= NA_KH and rows % NA_ROWS_PER_STEP == 0

    bf = lambda a: a.astype(BF16)
    vec = lambda a: a.reshape(depth, 1, -1).astype(F32)

    w_in, w_up, w_down, w_out = bf(w_in), bf(w_up), bf(w_down), bf(w_out)
    w_br_a, w_br_b, w_br_m = bf(w_br_a), bf(w_br_b), bf(w_br_m)
    g_pre_mix, g_post_mix, g_pre_ffn, g_post_ffn = map(vec, (g_pre_mix, g_post_mix, g_pre_ffn,
                                                             g_post_ffn))
    b_gate, conv_b, conv_w = vec(b_gate), vec(conv_b), conv_w.astype(F32)
    tab = _na_bias_table(rpb_na, rows)
    kv = _memkv(mem, mem_norm_g.reshape(1, -1).astype(F32), bf(w_mem_kv))

    x2 = x.reshape(b * s, d)
    for l in range(depth):
        qa, ka, va, b0, b1, b2, qm, gates = _inproj(x2, g_pre_mix, w_in, b_gate, l, b, s)
        oa = _na_attention(qa, ka, va, tab, l, b, s)
        ob, lse = zip(*(_dil_attention(q, grp) for grp, q in enumerate((b0, b1, b2))))
        x2 = _merge(x2, oa, ob, lse, qm, gates, kv, w_br_a, w_br_b, w_br_m, w_out, g_post_mix, l, s)
        x2 = _ffn(x2, g_pre_ffn, w_up, conv_w, conv_b, w_down, g_post_ffn, l, s)
    return x2.reshape(b, s, d)
```

```python
import functools

import jax
import jax.numpy as jnp
from jax import lax
from jax.experimental import pallas as pl
from jax.experimental.pallas import tpu as pltpu

D_MODEL = 1024
GRID_W = 64
N_MEM = 256
HEAD_DIM = 64
NA_HEADS = 6
NA_KH = 8
NA_KW = 16
DIL_PAIRS = ((128, 1), (512, 4), (2048, 16))
DIL_HEADS = 6
MEM_HEADS = 4
D_A = 384
D_B = 384
D_B_OUT = 128
D_M = 256
N_BRANCH = 3
D_IN = 3 * D_A + 3 * D_B + D_M + N_BRANCH * D_MODEL
D_FF = 2816
RMS_EPS = 1e-6
NEG_INF = -1e30

LANES = 128
SUB = 8
HALO = 16
VMEM_LIMIT = 56 * 1024 * 1024
ROW_TILE = 512

F32 = jnp.float32
BF16 = jnp.bfloat16


def _rms(x, g):
    return x * lax.rsqrt(jnp.mean(x * x, axis=-1, keepdims=True) + RMS_EPS) * g


def _dot(a, b):
    return jnp.dot(a, b, preferred_element_type=F32)


def _dot_nt(a, b):
    return lax.dot_general(a, b, (((1,), (1,)), ((), ())), preferred_element_type=F32)


def _layer_param(shape, layer):
    zeros = (0,) * len(shape)
    return pl.BlockSpec((1,) + tuple(shape), lambda *_: (layer,) + zeros,
                        pipeline_mode=pl.Buffered(1))


def _params(n_axes=1):
    return pltpu.CompilerParams(dimension_semantics=("arbitrary",) * n_axes,
                                vmem_limit_bytes=VMEM_LIMIT)


def _memkv_kernel(mem_ref, g_ref, w_ref, o_ref):
    mn = _rms(mem_ref[0], g_ref[...]).astype(BF16)
    o_ref[0, 0] = _dot(mn, w_ref[0]).astype(BF16)


def _memkv(mem, g, w_kv):
    depth = w_kv.shape[0]
    b = mem.shape[0]
    return pl.pallas_call(
        _memkv_kernel,
        grid=(depth, b),
        in_specs=[
            pl.BlockSpec((1, N_MEM, D_MODEL), lambda l, i: (i, 0, 0)),
            pl.BlockSpec((1, D_MODEL), lambda l, i: (0, 0)),
            pl.BlockSpec((1, D_MODEL, 2 * D_M), lambda l, i: (l, 0, 0)),
        ],
        out_specs=pl.BlockSpec((1, 1, N_MEM, 2 * D_M), lambda l, i: (l, i, 0, 0)),
        out_shape=jax.ShapeDtypeStruct((depth, b, N_MEM, 2 * D_M), BF16),
        compiler_params=_params(2),
        name="memkv",
    )(mem, g, w_kv)


IN_CHUNK = 512
_A_SLABS = D_A // LANES
_M_SLABS = D_M // LANES
_N_GROUPS = len(DIL_PAIRS)


def _inproj_kernel(x_ref, g_ref, w_ref, bg_ref, qa_ref, ka_ref, va_ref, b0_ref, b1_ref, b2_ref,
                   qm_ref, gate_ref, stage_ref):
    h = _rms(x_ref[...], g_ref[0]).astype(BF16)
    scale = HEAD_DIM ** -0.5
    b_refs = (b0_ref, b1_ref, b2_ref)

    def put_dilated(which, grp, val):
        d = DIL_PAIRS[grp][1]
        cs = slice(which * LANES, (which + 1) * LANES)
        if d == 1:
            b_refs[grp][0, 0, :, cs] = val.astype(BF16)
            return
        stage = stage_ref.at[(grp - 1) * 3 + which]
        stage[...] = val
        for r in range(d):
            b_refs[grp][0, r, :, cs] = stage[pl.ds(r, ROW_TILE // d, stride=d), :].astype(BF16)

    def put(slab, val):
        s = slab
        if s < 3 * _A_SLABS:
            which, j = divmod(s, _A_SLABS)
            ref = (qa_ref, ka_ref, va_ref)[which]
            if which == 0:
                val = val * scale
            ref[:, j * LANES:(j + 1) * LANES] = val.astype(BF16)
            return
        s -= 3 * _A_SLABS
        if s < 3 * _N_GROUPS:
            which, grp = divmod(s, _N_GROUPS)
            put_dilated(which, grp, val * scale if which == 0 else val)
            return
        s -= 3 * _N_GROUPS
        if s < _M_SLABS:
            qm_ref[:, s * LANES:(s + 1) * LANES] = (val * scale).astype(BF16)
            return
        s -= _M_SLABS
        gate = jax.nn.sigmoid(val + bg_ref[0, :, s * LANES:(s + 1) * LANES])
        gate_ref[:, s * LANES:(s + 1) * LANES] = gate.astype(BF16)

    per = IN_CHUNK // LANES
    for c in range(D_IN // IN_CHUNK):
        r = _dot(h, w_ref[0, :, c * IN_CHUNK:(c + 1) * IN_CHUNK])
        for j in range(per):
            put(c * per + j, r[:, j * LANES:(j + 1) * LANES])


def _inproj(x2, g, w_in, b_gate, layer, b, s):
    t = x2.shape[0]
    tiles_per_seq = s // ROW_TILE
    row = lambda i: (i, 0)
    outs = [jax.ShapeDtypeStruct((t, D_A), BF16)] * 3
    out_specs = [pl.BlockSpec((ROW_TILE, D_A), row)] * 3
    for _, d in DIL_PAIRS:
        assert ROW_TILE % (d * HALO) == 0
        outs.append(jax.ShapeDtypeStruct((b, d, s // d, 3 * LANES), BF16))
        out_specs.append(pl.BlockSpec((1, d, ROW_TILE // d, 3 * LANES),
                                      lambda i: (i // tiles_per_seq, 0, i % tiles_per_seq, 0)))
    outs += [jax.ShapeDtypeStruct((t, D_M), BF16),
             jax.ShapeDtypeStruct((t, N_BRANCH * D_MODEL), BF16)]
    out_specs += [pl.BlockSpec((ROW_TILE, D_M), row),
                  pl.BlockSpec((ROW_TILE, N_BRANCH * D_MODEL), row)]
    return pl.pallas_call(
        _inproj_kernel,
        grid=(t // ROW_TILE,),
        in_specs=[
            pl.BlockSpec((ROW_TILE, D_MODEL), row),
            _layer_param((1, D_MODEL), layer),
            _layer_param((D_MODEL, D_IN), layer),
            _layer_param((1, N_BRANCH * D_MODEL), layer),
        ],
        out_specs=out_specs,
        out_shape=outs,
        scratch_shapes=[pltpu.VMEM((3 * (_N_GROUPS - 1), ROW_TILE, LANES), F32)],
        compiler_params=_params(),
        name="inproj",
    )(x2, g, w_in, b_gate)


def _pair_scores(q2, k2):
    lane = lax.broadcasted_iota(jnp.int32, q2.shape, 1)
    zero = jnp.zeros_like(q2)
    qs = jnp.concatenate([jnp.where(lane < HEAD_DIM, q2, zero),
                          jnp.where(lane < HEAD_DIM, zero, q2)], axis=0)
    return _dot_nt(qs, k2)


def _softmax_parts(s):
    m = jnp.max(s, axis=-1, keepdims=True)
    p = jnp.exp(s - m)
    return m, p.astype(BF16), jnp.sum(p, axis=-1, keepdims=True)


def _unstack_heads(a):
    m_rows = a.shape[0] // 2
    lane = lax.broadcasted_iota(jnp.int32, (m_rows, LANES), 1)
    return jnp.where(lane < HEAD_DIM, a[:m_rows], a[m_rows:])


NA_ROWS_PER_STEP = 8
NA_GROUP = 4
NA_BAND = NA_KH * GRID_W


NA_RPB_PAD = GRID_W - NA_KW


def _na_build_bias(rp_ref, toe_ref, tab_ref):
    lane = lax.broadcasted_iota(jnp.int32, (GRID_W, LANES), 1)
    q = lax.broadcasted_iota(jnp.int32, (GRID_W, LANES), 0)
    c = lane % GRID_W
    c0 = jnp.clip(q - NA_KW // 2, 0, GRID_W - NA_KW)
    valid = (c >= c0) & (c < c0 + NA_KW)
    base_lo, base_hi = LANES - (GRID_W - 1), LANES - (GRID_W - 1) - GRID_W

    def toe_body(ro, carry):
        for h in range(NA_HEADS):
            row = jnp.broadcast_to(rp_ref[0, h, pl.ds(ro, 1), :], (GRID_W, LANES))
            lo = pltpu.roll(row, base_lo, 1, stride=1, stride_axis=0)
            hi = pltpu.roll(row, base_hi, 1, stride=1, stride_axis=0)
            toe_ref[h, ro] = jnp.where(valid, jnp.where(lane < GRID_W, lo, hi), NEG_INF)
        return carry

    lax.fori_loop(0, 2 * NA_KH - 1, toe_body, 0)

    def tab_body(dl, carry):
        for h in range(NA_HEADS):
            rows_ = slice((h % 2) * GRID_W, (h % 2 + 1) * GRID_W)
            for g in range(NA_KH // 2):
                even, odd = toe_ref[h, dl + 2 * g], toe_ref[h, dl + 2 * g + 1]
                tab_ref[h // 2, dl, rows_, g * LANES:(g + 1) * LANES] = \
                    jnp.where(lane < GRID_W, even, odd)
        return carry

    lax.fori_loop(0, NA_KH, tab_body, 0)


def _na_kernel(q_ref, k_ref, v_ref, rp_ref, o_ref, toe_ref, tab_ref, *, rows):
    rb = pl.program_id(1)

    @pl.when((pl.program_id(0) == 0) & (rb == 0))
    def _():
        _na_build_bias(rp_ref, toe_ref, tab_ref)

    def group_body(gi, carry):
        chains = []
        for lg in range(NA_GROUP):
            lr = gi * NA_GROUP + lg
            i = rb * NA_ROWS_PER_STEP + lr
            r0 = jnp.clip(i - NA_KH // 2, 0, rows - NA_KH)
            ks = pl.multiple_of(r0 * GRID_W, GRID_W)
            qs = pl.multiple_of(lr * GRID_W, GRID_W)
            for hp in range(NA_HEADS // 2):
                chains.append((qs, ks, r0 - i + (NA_KH - 1), hp,
                               slice(hp * LANES, (hp + 1) * LANES)))
        ss = [_pair_scores(q_ref[0, pl.ds(qs, GRID_W), cs], k_ref[0, pl.ds(ks, NA_BAND), cs])
              for qs, ks, dl, hp, cs in chains]
        parts = [_softmax_parts(s + tab_ref[hp, dl])
                 for s, (qs, ks, dl, hp, cs) in zip(ss, chains)]
        for (_, p, den), (qs, ks, dl, hp, cs) in zip(parts, chains):
            pv = _dot(p, v_ref[0, pl.ds(ks, NA_BAND), cs])
            o_ref[0, pl.ds(qs, GRID_W), cs] = _unstack_heads(pv / den).astype(BF16)
        return carry

    lax.fori_loop(0, NA_ROWS_PER_STEP // NA_GROUP, group_body, 0)


def _na_attention(q, k, v, rp, layer, b, s):
    rows = s // GRID_W
    qblk = NA_ROWS_PER_STEP * GRID_W
    q3, k3, v3 = (a.reshape(b, s, D_A) for a in (q, k, v))
    out = pl.pallas_call(
        functools.partial(_na_kernel, rows=rows),
        grid=(b, rows // NA_ROWS_PER_STEP),
        in_specs=[
            pl.BlockSpec((1, qblk, D_A), lambda i, j: (i, j, 0)),
            pl.BlockSpec((1, s, D_A), lambda i, j: (i, 0, 0)),
            pl.BlockSpec((1, s, D_A), lambda i, j: (i, 0, 0)),
            _layer_param(rp.shape[1:], layer),
        ],
        out_specs=pl.BlockSpec((1, qblk, D_A), lambda i, j: (i, j, 0)),
        out_shape=jax.ShapeDtypeStruct((b, s, D_A), BF16),
        scratch_shapes=[pltpu.VMEM((NA_HEADS, 2 * NA_KH - 1, GRID_W, LANES), F32),
                        pltpu.VMEM((NA_HEADS // 2, NA_KH, 2 * GRID_W, NA_BAND), F32)],
        compiler_params=_params(2),
        name="na_attn",
    )(q3, k3, v3, rp)
    return out.reshape(b * s, D_A)


DIL_QB = 128
DIL_HALF = 64
DIL_KB = DIL_QB + 2 * DIL_HALF
DIL_GROUP = 8
DIL_KEY_SHIFTS = (0, -1, -2)


def _alibi_slope(head):
    return 2.0 ** (-8.0 * (head + 1) / DIL_HEADS)


def _dil_kernel(x_ref, o_ref, lse_ref, bias_ref, *, length, dilation, slopes, residues):
    nblk = length // DIL_QB

    @pl.when((pl.program_id(0) == 0) & (pl.program_id(1) == 0))
    def _():
        rel0 = (lax.broadcasted_iota(jnp.int32, (DIL_QB, DIL_KB), 1)
                - lax.broadcasted_iota(jnp.int32, (DIL_QB, DIL_KB), 0))
        for c, shift in enumerate(DIL_KEY_SHIFTS):
            dist = jnp.abs(rel0 + shift * DIL_HALF)
            dist_f = (dist * dilation).astype(F32)
            for hh in range(2):
                bias = jnp.where(dist <= DIL_HALF, (-slopes[hh]) * dist_f, NEG_INF)
                bias_ref[c, hh * DIL_QB:(hh + 1) * DIL_QB, :] = bias

    def group(chains):
        geo = []
        for r, bi in chains:
            q0 = bi * DIL_QB
            if isinstance(bi, int):
                ks = min(max(q0 - DIL_HALF, 0), length - DIL_KB)
            else:
                q0 = pl.multiple_of(q0, DIL_QB)
                ks = pl.multiple_of(jnp.clip(q0 - DIL_HALF, 0, length - DIL_KB), DIL_HALF)
            geo.append((r, q0, ks, (q0 - ks) // DIL_HALF))
        ss = [_pair_scores(x_ref[0, r, pl.ds(q0, DIL_QB), 0:LANES],
                           x_ref[0, r, pl.ds(ks, DIL_KB), LANES:2 * LANES])
              for r, q0, ks, case in geo]
        parts = [_softmax_parts(s + bias_ref[case]) for s, (r, q0, ks, case) in zip(ss, geo)]
        for (m, p, den), (r, q0, ks, case) in zip(parts, geo):
            pv = _dot(p, x_ref[0, r, pl.ds(ks, DIL_KB), 2 * LANES:3 * LANES])
            o_ref[0, r, pl.ds(q0, DIL_QB), :] = _unstack_heads(pv / den).astype(BF16)
            lse = jnp.broadcast_to(m + jnp.log(den), (2 * DIL_QB, LANES))
            lse_ref[0, r, pl.ds(q0, DIL_QB), :] = _unstack_heads(lse)

    if residues * nblk <= DIL_GROUP:
        group([(r, bi) for r in range(residues) for bi in range(nblk)])
    else:
        assert nblk % DIL_GROUP == 0
        for r in range(residues):
            def body(gi, carry, r=r):
                group([(r, gi * DIL_GROUP + j) for j in range(DIL_GROUP)])
                return carry
            lax.fori_loop(0, nblk // DIL_GROUP, body, 0)


def _dil_attention(qkv, grp):
    window, dilation = DIL_PAIRS[grp]
    assert window // 2 // dilation == DIL_HALF
    b, _, length, width = qkv.shape
    assert length % DIL_QB == 0 and length >= DIL_KB
    nblk = length // DIL_QB
    residues = min(dilation, max(1, DIL_GROUP // nblk))
    assert dilation % residues == 0
    slopes = tuple(_alibi_slope(2 * grp + hh) for hh in range(2))
    blk = lambda w: pl.BlockSpec((1, residues, length, w), lambda i, r: (i, r, 0, 0))
    return pl.pallas_call(
        functools.partial(_dil_kernel, length=length, dilation=dilation, slopes=slopes,
                          residues=residues),
        grid=(b, dilation // residues),
        in_specs=[blk(width)],
        out_specs=[blk(LANES), blk(LANES)],
        out_shape=[jax.ShapeDtypeStruct((b, dilation, length, LANES), BF16),
                   jax.ShapeDtypeStruct((b, dilation, length, LANES), F32)],
        scratch_shapes=[pltpu.VMEM((len(DIL_KEY_SHIFTS), 2 * DIL_QB, DIL_KB), F32)],
        compiler_params=_params(2),
        name=f"dil_attn_{grp}",
    )(qkv)


def _merge_kernel(x_ref, oa_ref, o0_ref, o1_ref, o2_ref, l0_ref, l1_ref, l2_ref, qm_ref, gate_ref,
                  kv_ref, wa_ref, wb_ref, wm_ref, wo_ref, g_ref, out_ref, stage_ref):
    om = []
    for hp in range(MEM_HEADS // 2):
        s = _pair_scores(qm_ref[:, hp * LANES:(hp + 1) * LANES],
                         kv_ref[0, 0, :, hp * LANES:(hp + 1) * LANES])
        _, p, den = _softmax_parts(s)
        pv = _dot(p, kv_ref[0, 0, :, D_M + hp * LANES:D_M + (hp + 1) * LANES])
        om.append(_unstack_heads(pv / den).astype(BF16))
    o_m = jnp.concatenate(om, axis=-1)

    def natural_order(ref, grp, slot):
        d = DIL_PAIRS[grp][1]
        if d == 1:
            return ref[0, 0].astype(F32)
        stage = stage_ref.at[slot]
        for r in range(d):
            stage[pl.ds(r, ROW_TILE // d, stride=d), :] = ref[0, r].astype(F32)
        return stage[...]

    o_refs, l_refs = (o0_ref, o1_ref, o2_ref), (l0_ref, l1_ref, l2_ref)
    os_ = [natural_order(o_refs[g], g, 2 * (g - 1)) for g in range(_N_GROUPS)]
    ls = [natural_order(l_refs[g], g, 2 * (g - 1) + 1) for g in range(_N_GROUPS)]
    mx = jnp.maximum(jnp.maximum(ls[0], ls[1]), ls[2])
    es = [jnp.exp(l - mx) for l in ls]
    inv = 1.0 / (es[0] + es[1] + es[2])
    o_b = ((es[0] * inv) * os_[0] + (es[1] * inv) * os_[1] + (es[2] * inv) * os_[2]).astype(BF16)

    merged = (gate_ref[:, 0:D_MODEL].astype(F32) * _dot(oa_ref[...], wa_ref[0])
              + gate_ref[:, D_MODEL:2 * D_MODEL].astype(F32) * _dot(o_b, wb_ref[0])
              + gate_ref[:, 2 * D_MODEL:3 * D_MODEL].astype(F32) * _dot(o_m, wm_ref[0]))
    y = _dot(merged.astype(BF16), wo_ref[0])
    out_ref[...] = x_ref[...] + _rms(y, g_ref[0])


def _merge(x2, oa, ob, lse, qm, gates, kv, wa, wb, wm, wo, g, layer, s):
    t = x2.shape[0]
    tiles_per_seq = s // ROW_TILE
    row = lambda i: (i, 0)
    dil = [pl.BlockSpec((1, d, ROW_TILE // d, LANES),
                        lambda i: (i // tiles_per_seq, 0, i % tiles_per_seq, 0))
           for _, d in DIL_PAIRS]
    in_specs = [
        pl.BlockSpec((ROW_TILE, D_MODEL), row),
        pl.BlockSpec((ROW_TILE, D_A), row),
        *dil, *dil,
        pl.BlockSpec((ROW_TILE, D_M), row),
        pl.BlockSpec((ROW_TILE, N_BRANCH * D_MODEL), row),
        pl.BlockSpec((1, 1, N_MEM, 2 * D_M), lambda i: (layer, i // tiles_per_seq, 0, 0)),
        _layer_param((D_A, D_MODEL), layer), _layer_param((D_B_OUT, D_MODEL), layer),
        _layer_param((D_M, D_MODEL), layer), _layer_param((D_MODEL, D_MODEL), layer),
        _layer_param((1, D_MODEL), layer),
    ]
    return pl.pallas_call(
        _merge_kernel,
        grid=(t // ROW_TILE,),
        in_specs=in_specs,
        out_specs=pl.BlockSpec((ROW_TILE, D_MODEL), row),
        out_shape=jax.ShapeDtypeStruct((t, D_MODEL), F32),
        scratch_shapes=[pltpu.VMEM((2 * (_N_GROUPS - 1), ROW_TILE, LANES), F32)],
        compiler_params=_params(),
        name="merge",
    )(x2, oa, *ob, *lse, qm, gates, kv, wa, wb, wm, wo, g)


FF_CHUNK = 256


def _ffn_kernel(xm_ref, xp_ref, xn_ref, g1_ref, wup_ref, cw_ref, cb_ref, wdn_ref, g2_ref, out_ref,
                hext_ref, f_ref, *, tiles_per_seq):
    i = pl.program_id(0)
    pos = i % tiles_per_seq
    g1 = g1_ref[0]
    x = xm_ref[...]
    hext_ref[0:ROW_TILE, :] = _rms(x, g1).astype(BF16)
    hn = jnp.where(pos == tiles_per_seq - 1, 0.0, _rms(xn_ref[...], g1))
    hp = jnp.where(pos == 0, 0.0, _rms(xp_ref[...], g1))
    hext_ref[ROW_TILE:ROW_TILE + 2 * SUB, :] = jnp.concatenate([hn, hp], axis=0).astype(BF16)
    hext = hext_ref[...]
    ext = ROW_TILE + 2 * SUB

    def conv(u, col):
        cs = slice(col, col + FF_CHUNK)
        before = pltpu.roll(u, 1, 0)[0:ROW_TILE]
        after = pltpu.roll(u, ext - 1, 0)[0:ROW_TILE]
        y = cb_ref[0, :, cs] + before * cw_ref[0, 0:1, cs]
        y = y + u[0:ROW_TILE] * cw_ref[0, 1:2, cs]
        return y + after * cw_ref[0, 2:3, cs]

    for c in range(D_FF // FF_CHUNK):
        ca, cb = c * FF_CHUNK, D_FF + c * FF_CHUNK
        ua = conv(_dot(hext, wup_ref[0, :, ca:ca + FF_CHUNK]), ca)
        ub = conv(_dot(hext, wup_ref[0, :, cb:cb + FF_CHUNK]), cb)
        f_ref[:, ca:ca + FF_CHUNK] = (jax.nn.gelu(ua) * ub).astype(BF16)

    y = _dot(f_ref[...], wdn_ref[0])
    out_ref[...] = x + _rms(y, g2_ref[0])


def _ffn(x2, g1, w_up, cw, cb, w_dn, g2, layer, s):
    t = x2.shape[0]
    tiles_per_seq = s // ROW_TILE
    per = ROW_TILE // SUB
    nh = t // SUB
    row = lambda i: (i, 0)
    return pl.pallas_call(
        functools.partial(_ffn_kernel, tiles_per_seq=tiles_per_seq),
        grid=(t // ROW_TILE,),
        in_specs=[
            pl.BlockSpec((ROW_TILE, D_MODEL), row),
            pl.BlockSpec((SUB, D_MODEL), lambda i: (jnp.maximum(i * per - 1, 0), 0)),
            pl.BlockSpec((SUB, D_MODEL), lambda i: (jnp.minimum((i + 1) * per, nh - 1), 0)),
            _layer_param((1, D_MODEL), layer),
            _layer_param((D_MODEL, 2 * D_FF), layer),
            _layer_param((3, 2 * D_FF), layer),
            _layer_param((1, 2 * D_FF), layer),
            _layer_param((D_FF, D_MODEL), layer),
            _layer_param((1, D_MODEL), layer),
        ],
        out_specs=pl.BlockSpec((ROW_TILE, D_MODEL), row),
        out_shape=jax.ShapeDtypeStruct((t, D_MODEL), F32),
        scratch_shapes=[pltpu.VMEM((ROW_TILE + 2 * SUB, D_MODEL), BF16),
                        pltpu.VMEM((ROW_TILE, D_FF), BF16)],
        compiler_params=_params(),
        name="ffn",
    )(x2, x2, x2, g1, w_up, cw, cb, w_dn, g2)


def kernel(x, mem, mem_norm_g, g_pre_mix, w_in, rpb_na, w_mem_kv, b_gate, w_br_a, w_br_b, w_br_m,
           w_out, g_post_mix, g_pre_ffn, w_up, conv_w, conv_b, w_down, g_post_ffn):
    b, s, d = x.shape
    depth = w_in.shape[0]
    assert d == D_MODEL and s % GRID_W == 0 and s % ROW_TILE == 0
    rows = s // GRID_W
    assert rows >= NA_KH and rows % NA_ROWS_PER_STEP == 0

    bf = lambda a: a.astype(BF16)
    vec = lambda a: a.reshape(depth, 1, -1).astype(F32)

    w_in, w_up, w_down, w_out = bf(w_in), bf(w_up), bf(w_down), bf(w_out)
    w_br_a, w_br_b, w_br_m = bf(w_br_a), bf(w_br_b), bf(w_br_m)
    g_pre_mix, g_post_mix, g_pre_ffn, g_post_ffn = map(vec, (g_pre_mix, g_post_mix, g_pre_ffn,
                                                             g_post_ffn))
    b_gate, conv_b, conv_w = vec(b_gate), vec(conv_b), conv_w.astype(F32)
    assert rpb_na.shape[1:] == (NA_HEADS, 2 * NA_KH - 1, 2 * NA_KW - 1)
    rp = jnp.pad(rpb_na.astype(F32), ((0, 0), (0, 0), (0, 0),
                                      (NA_RPB_PAD, LANES - NA_RPB_PAD - (2 * NA_KW - 1))))
    kv = _memkv(mem, mem_norm_g.reshape(1, -1).astype(F32), bf(w_mem_kv))

    x2 = x.reshape(b * s, d)
    for l in range(depth):
        qa, ka, va, b0, b1, b2, qm, gates = _inproj(x2, g_pre_mix, w_in, b_gate, l, b, s)
        oa = _na_attention(qa, ka, va, rp, l, b, s)
        ob, lse = zip(*(_dil_attention(q, grp) for grp, q in enumerate((b0, b1, b2))))
        x2 = _merge(x2, oa, ob, lse, qm, gates, kv, w_br_a, w_br_b, w_br_m, w_out, g_post_mix, l, s)
        x2 = _ffn(x2, g_pre_ffn, w_up, conv_w, conv_b, w_down, g_post_ffn, l, s)
    return x2.reshape(b, s, d)
```

```python
import functools

import jax
import jax.numpy as jnp
from jax import lax
from jax.experimental import pallas as pl
from jax.experimental.pallas import tpu as pltpu

D_MODEL = 1024
GRID_W = 64
N_MEM = 256
HEAD_DIM = 64
NA_HEADS = 6
NA_KH = 8
NA_KW = 16
DIL_PAIRS = ((128, 1), (512, 4), (2048, 16))
DIL_HEADS = 6
MEM_HEADS = 4
D_A = 384
D_B = 384
D_B_OUT = 128
D_M = 256
N_BRANCH = 3
D_IN = 3 * D_A + 3 * D_B + D_M + N_BRANCH * D_MODEL
D_FF = 2816
RMS_EPS = 1e-6
NEG_INF = -1e30

LANES = 128
SUB = 8
HALO = 16
VMEM_LIMIT = 56 * 1024 * 1024
IN_TM = 1024
ROW_TILE = 512
FF_TM = 1024

F32 = jnp.float32
BF16 = jnp.bfloat16

LOG2E = 1.4426950408889634
LN2 = 0.6931471805599453
Q_SCALE = HEAD_DIM ** -0.5 * LOG2E


def _rms(x, g):
    return x * lax.rsqrt(jnp.mean(x * x, axis=-1, keepdims=True) + RMS_EPS) * g


def _dot(a, b):
    return jnp.dot(a, b, preferred_element_type=F32)


def _dot_nt(a, b):
    return lax.dot_general(a, b, (((1,), (1,)), ((), ())), preferred_element_type=F32)


def _layer_param(shape, layer):
    zeros = (0,) * len(shape)
    return pl.BlockSpec((1,) + tuple(shape), lambda *_: (layer,) + zeros,
                        pipeline_mode=pl.Buffered(1))


def _params(n_axes=1):
    return pltpu.CompilerParams(dimension_semantics=("arbitrary",) * n_axes,
                                vmem_limit_bytes=VMEM_LIMIT)


def _memkv_kernel(mem_ref, g_ref, w_ref, o_ref):
    mn = _rms(mem_ref[0], g_ref[...]).astype(BF16)
    o_ref[0, 0] = _dot(mn, w_ref[0]).astype(BF16)


def _memkv(mem, g, w_kv):
    depth = w_kv.shape[0]
    b = mem.shape[0]
    return pl.pallas_call(
        _memkv_kernel,
        grid=(depth, b),
        in_specs=[
            pl.BlockSpec((1, N_MEM, D_MODEL), lambda l, i: (i, 0, 0)),
            pl.BlockSpec((1, D_MODEL), lambda l, i: (0, 0)),
            pl.BlockSpec((1, D_MODEL, 2 * D_M), lambda l, i: (l, 0, 0)),
        ],
        out_specs=pl.BlockSpec((1, 1, N_MEM, 2 * D_M), lambda l, i: (l, i, 0, 0)),
        out_shape=jax.ShapeDtypeStruct((depth, b, N_MEM, 2 * D_M), BF16),
        compiler_params=_params(2),
        name="memkv",
    )(mem, g, w_kv)


IN_CHUNK = 512
_A_SLABS = D_A // LANES
_M_SLABS = D_M // LANES
_N_GROUPS = len(DIL_PAIRS)


def _inproj_kernel(x_ref, g_ref, w_ref, bg_ref, qa_ref, ka_ref, va_ref, b0_ref, b1_ref, b2_ref,
                   qm_ref, gate_ref, stage_ref):
    h = _rms(x_ref[...], g_ref[0]).astype(BF16)
    scale = Q_SCALE
    b_refs = (b0_ref, b1_ref, b2_ref)

    def put_dilated(which, grp, val):
        d = DIL_PAIRS[grp][1]
        cs = slice(which * LANES, (which + 1) * LANES)
        if d == 1:
            b_refs[grp][0, 0, :, cs] = val.astype(BF16)
            return
        stage = stage_ref.at[(grp - 1) * 3 + which]
        stage[...] = val
        for r in range(d):
            b_refs[grp][0, r, :, cs] = stage[pl.ds(r, IN_TM // d, stride=d), :].astype(BF16)

    def put(slab, val):
        s = slab
        if s < 3 * _A_SLABS:
            which, j = divmod(s, _A_SLABS)
            ref = (qa_ref, ka_ref, va_ref)[which]
            if which == 0:
                val = val * scale
            ref[:, j * LANES:(j + 1) * LANES] = val.astype(BF16)
            return
        s -= 3 * _A_SLABS
        if s < 3 * _N_GROUPS:
            which, grp = divmod(s, _N_GROUPS)
            put_dilated(which, grp, val * scale if which == 0 else val)
            return
        s -= 3 * _N_GROUPS
        if s < _M_SLABS:
            qm_ref[:, s * LANES:(s + 1) * LANES] = (val * scale).astype(BF16)
            return
        s -= _M_SLABS
        gate = jax.nn.sigmoid(val + bg_ref[0, :, s * LANES:(s + 1) * LANES])
        gate_ref[:, s * LANES:(s + 1) * LANES] = gate.astype(BF16)

    per = IN_CHUNK // LANES
    for c in range(D_IN // IN_CHUNK):
        r = _dot(h, w_ref[0, :, c * IN_CHUNK:(c + 1) * IN_CHUNK])
        for j in range(per):
            put(c * per + j, r[:, j * LANES:(j + 1) * LANES])


def _inproj(x2, g, w_in, b_gate, layer, b, s):
    t = x2.shape[0]
    assert s % IN_TM == 0
    tiles_per_seq = s // IN_TM
    row = lambda i: (i, 0)
    outs = [jax.ShapeDtypeStruct((t, D_A), BF16)] * 3
    out_specs = [pl.BlockSpec((IN_TM, D_A), row)] * 3
    for _, d in DIL_PAIRS:
        assert IN_TM % (d * HALO) == 0
        outs.append(jax.ShapeDtypeStruct((b, d, s // d, 3 * LANES), BF16))
        out_specs.append(pl.BlockSpec((1, d, IN_TM // d, 3 * LANES),
                                      lambda i: (i // tiles_per_seq, 0, i % tiles_per_seq, 0)))
    outs += [jax.ShapeDtypeStruct((t, D_M), BF16),
             jax.ShapeDtypeStruct((t, N_BRANCH * D_MODEL), BF16)]
    out_specs += [pl.BlockSpec((IN_TM, D_M), row),
                  pl.BlockSpec((IN_TM, N_BRANCH * D_MODEL), row)]
    return pl.pallas_call(
        _inproj_kernel,
        grid=(t // IN_TM,),
        in_specs=[
            pl.BlockSpec((IN_TM, D_MODEL), row),
            _layer_param((1, D_MODEL), layer),
            _layer_param((D_MODEL, D_IN), layer),
            _layer_param((1, N_BRANCH * D_MODEL), layer),
        ],
        out_specs=out_specs,
        out_shape=outs,
        scratch_shapes=[pltpu.VMEM((3 * (_N_GROUPS - 1), IN_TM, LANES), F32)],
        compiler_params=_params(),
        name="inproj",
    )(x2, g, w_in, b_gate)


def _pair_scores(q2, k2):
    lane = lax.broadcasted_iota(jnp.int32, q2.shape, 1)
    zero = jnp.zeros_like(q2)
    qs = jnp.concatenate([jnp.where(lane < HEAD_DIM, q2, zero),
                          jnp.where(lane < HEAD_DIM, zero, q2)], axis=0)
    return _dot_nt(qs, k2)


def _softmax_parts(s):
    m = jnp.max(s, axis=-1, keepdims=True)
    p = jnp.exp2(s - m)
    return m, p.astype(BF16), jnp.sum(p, axis=-1, keepdims=True)


def _unstack_heads(a):
    m_rows = a.shape[0] // 2
    lane = lax.broadcasted_iota(jnp.int32, (m_rows, LANES), 1)
    return jnp.where(lane < HEAD_DIM, a[:m_rows], a[m_rows:])


NA_ROWS_PER_STEP = 8
NA_GROUP = 4
NA_BAND = NA_KH * GRID_W


NA_RPB_PAD = GRID_W - NA_KW


def _na_build_bias(rp_ref, toe_ref, tab_ref):
    lane = lax.broadcasted_iota(jnp.int32, (GRID_W, LANES), 1)
    q = lax.broadcasted_iota(jnp.int32, (GRID_W, LANES), 0)
    c = lane % GRID_W
    c0 = jnp.clip(q - NA_KW // 2, 0, GRID_W - NA_KW)
    valid = (c >= c0) & (c < c0 + NA_KW)
    base_lo, base_hi = LANES - (GRID_W - 1), LANES - (GRID_W - 1) - GRID_W

    def toe_body(ro, carry):
        for h in range(NA_HEADS):
            row = jnp.broadcast_to(rp_ref[0, h, pl.ds(ro, 1), :], (GRID_W, LANES))
            lo = pltpu.roll(row, base_lo, 1, stride=1, stride_axis=0)
            hi = pltpu.roll(row, base_hi, 1, stride=1, stride_axis=0)
            toe_ref[h, ro] = jnp.where(valid, jnp.where(lane < GRID_W, lo, hi), NEG_INF)
        return carry

    lax.fori_loop(0, 2 * NA_KH - 1, toe_body, 0)

    def tab_body(dl, carry):
        for h in range(NA_HEADS):
            rows_ = slice((h % 2) * GRID_W, (h % 2 + 1) * GRID_W)
            for g in range(NA_KH // 2):
                even, odd = toe_ref[h, dl + 2 * g], toe_ref[h, dl + 2 * g + 1]
                tab_ref[h // 2, dl, rows_, g * LANES:(g + 1) * LANES] = \
                    jnp.where(lane < GRID_W, even, odd)
        return carry

    lax.fori_loop(0, NA_KH, tab_body, 0)


def _na_kernel(q_ref, k_ref, v_ref, rp_ref, o_ref, toe_ref, tab_ref, *, rows):
    rb = pl.program_id(1)

    @pl.when((pl.program_id(0) == 0) & (rb == 0))
    def _():
        _na_build_bias(rp_ref, toe_ref, tab_ref)

    def group_body(gi, carry):
        chains = []
        for lg in range(NA_GROUP):
            lr = gi * NA_GROUP + lg
            i = rb * NA_ROWS_PER_STEP + lr
            r0 = jnp.clip(i - NA_KH // 2, 0, rows - NA_KH)
            ks = pl.multiple_of(r0 * GRID_W, GRID_W)
            qs = pl.multiple_of(lr * GRID_W, GRID_W)
            for hp in range(NA_HEADS // 2):
                chains.append((qs, ks, r0 - i + (NA_KH - 1), hp,
                               slice(hp * LANES, (hp + 1) * LANES)))
        ss = [_pair_scores(q_ref[0, pl.ds(qs, GRID_W), cs], k_ref[0, pl.ds(ks, NA_BAND), cs])
              for qs, ks, dl, hp, cs in chains]
        parts = [_softmax_parts(s + tab_ref[hp, dl])
                 for s, (qs, ks, dl, hp, cs) in zip(ss, chains)]
        for (_, p, den), (qs, ks, dl, hp, cs) in zip(parts, chains):
            pv = _dot(p, v_ref[0, pl.ds(ks, NA_BAND), cs])
            o_ref[0, pl.ds(qs, GRID_W), cs] = _unstack_heads(pv / den).astype(BF16)
        return carry

    lax.fori_loop(0, NA_ROWS_PER_STEP // NA_GROUP, group_body, 0)


def _na_attention(q, k, v, rp, layer, b, s):
    rows = s // GRID_W
    qblk = NA_ROWS_PER_STEP * GRID_W
    q3, k3, v3 = (a.reshape(b, s, D_A) for a in (q, k, v))
    out = pl.pallas_call(
        functools.partial(_na_kernel, rows=rows),
        grid=(b, rows // NA_ROWS_PER_STEP),
        in_specs=[
            pl.BlockSpec((1, qblk, D_A), lambda i, j: (i, j, 0)),
            pl.BlockSpec((1, s, D_A), lambda i, j: (i, 0, 0)),
            pl.BlockSpec((1, s, D_A), lambda i, j: (i, 0, 0)),
            _layer_param(rp.shape[1:], layer),
        ],
        out_specs=pl.BlockSpec((1, qblk, D_A), lambda i, j: (i, j, 0)),
        out_shape=jax.ShapeDtypeStruct((b, s, D_A), BF16),
        scratch_shapes=[pltpu.VMEM((NA_HEADS, 2 * NA_KH - 1, GRID_W, LANES), F32),
                        pltpu.VMEM((NA_HEADS // 2, NA_KH, 2 * GRID_W, NA_BAND), F32)],
        compiler_params=_params(2),
        name="na_attn",
    )(q3, k3, v3, rp)
    return out.reshape(b * s, D_A)


DIL_QB = 128
DIL_HALF = 64
DIL_KB = DIL_QB + 2 * DIL_HALF
DIL_GROUP = 8
DIL_KEY_SHIFTS = (0, -1, -2)


def _alibi_slope(head):
    return 2.0 ** (-8.0 * (head + 1) / DIL_HEADS)


def _dil_kernel(x_ref, o_ref, lse_ref, bias_ref, *, length, dilation, slopes, residues):
    nblk = length // DIL_QB

    @pl.when((pl.program_id(0) == 0) & (pl.program_id(1) == 0))
    def _():
        rel0 = (lax.broadcasted_iota(jnp.int32, (DIL_QB, DIL_KB), 1)
                - lax.broadcasted_iota(jnp.int32, (DIL_QB, DIL_KB), 0))
        for c, shift in enumerate(DIL_KEY_SHIFTS):
            dist = jnp.abs(rel0 + shift * DIL_HALF)
            dist_f = (dist * dilation).astype(F32)
            for hh in range(2):
                bias = jnp.where(dist <= DIL_HALF, (-slopes[hh] * LOG2E) * dist_f, NEG_INF)
                bias_ref[c, hh * DIL_QB:(hh + 1) * DIL_QB, :] = bias

    def group(chains):
        geo = []
        for r, bi in chains:
            q0 = bi * DIL_QB
            if isinstance(bi, int):
                ks = min(max(q0 - DIL_HALF, 0), length - DIL_KB)
            else:
                q0 = pl.multiple_of(q0, DIL_QB)
                ks = pl.multiple_of(jnp.clip(q0 - DIL_HALF, 0, length - DIL_KB), DIL_HALF)
            geo.append((r, q0, ks, (q0 - ks) // DIL_HALF))
        ss = [_pair_scores(x_ref[0, r, pl.ds(q0, DIL_QB), 0:LANES],
                           x_ref[0, r, pl.ds(ks, DIL_KB), LANES:2 * LANES])
              for r, q0, ks, case in geo]
        parts = [_softmax_parts(s + bias_ref[case]) for s, (r, q0, ks, case) in zip(ss, geo)]
        for (m, p, den), (r, q0, ks, case) in zip(parts, geo):
            pv = _dot(p, x_ref[0, r, pl.ds(ks, DIL_KB), 2 * LANES:3 * LANES])
            o_ref[0, r, pl.ds(q0, DIL_QB), :] = _unstack_heads(pv / den).astype(BF16)
            lse = jnp.broadcast_to(m * LN2 + jnp.log(den), (2 * DIL_QB, LANES))
            lse_ref[0, r, pl.ds(q0, DIL_QB), :] = _unstack_heads(lse)

    if residues * nblk <= DIL_GROUP:
        group([(r, bi) for r in range(residues) for bi in range(nblk)])
    else:
        assert nblk % DIL_GROUP == 0
        for r in range(residues):
            def body(gi, carry, r=r):
                group([(r, gi * DIL_GROUP + j) for j in range(DIL_GROUP)])
                return carry
            lax.fori_loop(0, nblk // DIL_GROUP, body, 0)


def _dil_attention(qkv, grp):
    window, dilation = DIL_PAIRS[grp]
    assert window // 2 // dilation == DIL_HALF
    b, _, length, width = qkv.shape
    assert length % DIL_QB == 0 and length >= DIL_KB
    nblk = length // DIL_QB
    residues = min(dilation, max(1, DIL_GROUP // nblk))
    assert dilation % residues == 0
    slopes = tuple(_alibi_slope(2 * grp + hh) for hh in range(2))
    blk = lambda w: pl.BlockSpec((1, residues, length, w), lambda i, r: (i, r, 0, 0))
    return pl.pallas_call(
        functools.partial(_dil_kernel, length=length, dilation=dilation, slopes=slopes,
                          residues=residues),
        grid=(b, dilation // residues),
        in_specs=[blk(width)],
        out_specs=[blk(LANES), blk(LANES)],
        out_shape=[jax.ShapeDtypeStruct((b, dilation, length, LANES), BF16),
                   jax.ShapeDtypeStruct((b, dilation, length, LANES), F32)],
        scratch_shapes=[pltpu.VMEM((len(DIL_KEY_SHIFTS), 2 * DIL_QB, DIL_KB), F32)],
        compiler_params=_params(2),
        name=f"dil_attn_{grp}",
    )(qkv)


def _merge_kernel(x_ref, oa_ref, o0_ref, o1_ref, o2_ref, l0_ref, l1_ref, l2_ref, qm_ref, gate_ref,
                  kv_ref, wa_ref, wb_ref, wm_ref, wo_ref, g_ref, out_ref, stage_ref):
    om = []
    for hp in range(MEM_HEADS // 2):
        s = _pair_scores(qm_ref[:, hp * LANES:(hp + 1) * LANES],
                         kv_ref[0, 0, :, hp * LANES:(hp + 1) * LANES])
        _, p, den = _softmax_parts(s)
        pv = _dot(p, kv_ref[0, 0, :, D_M + hp * LANES:D_M + (hp + 1) * LANES])
        om.append(_unstack_heads(pv / den).astype(BF16))
    o_m = jnp.concatenate(om, axis=-1)

    def natural_order(ref, grp, slot):
        d = DIL_PAIRS[grp][1]
        if d == 1:
            return ref[0, 0].astype(F32)
        stage = stage_ref.at[slot]
        for r in range(d):
            stage[pl.ds(r, ROW_TILE // d, stride=d), :] = ref[0, r].astype(F32)
        return stage[...]

    o_refs, l_refs = (o0_ref, o1_ref, o2_ref), (l0_ref, l1_ref, l2_ref)
    os_ = [natural_order(o_refs[g], g, 2 * (g - 1)) for g in range(_N_GROUPS)]
    ls = [natural_order(l_refs[g], g, 2 * (g - 1) + 1) for g in range(_N_GROUPS)]
    mx = jnp.maximum(jnp.maximum(ls[0], ls[1]), ls[2])
    es = [jnp.exp(l - mx) for l in ls]
    inv = 1.0 / (es[0] + es[1] + es[2])
    o_b = ((es[0] * inv) * os_[0] + (es[1] * inv) * os_[1] + (es[2] * inv) * os_[2]).astype(BF16)

    merged = (gate_ref[:, 0:D_MODEL].astype(F32) * _dot(oa_ref[...], wa_ref[0])
              + gate_ref[:, D_MODEL:2 * D_MODEL].astype(F32) * _dot(o_b, wb_ref[0])
              + gate_ref[:, 2 * D_MODEL:3 * D_MODEL].astype(F32) * _dot(o_m, wm_ref[0]))
    y = _dot(merged.astype(BF16), wo_ref[0])
    out_ref[...] = x_ref[...] + _rms(y, g_ref[0])


def _merge(x2, oa, ob, lse, qm, gates, kv, wa, wb, wm, wo, g, layer, s):
    t = x2.shape[0]
    tiles_per_seq = s // ROW_TILE
    row = lambda i: (i, 0)
    dil = [pl.BlockSpec((1, d, ROW_TILE // d, LANES),
                        lambda i: (i // tiles_per_seq, 0, i % tiles_per_seq, 0))
           for _, d in DIL_PAIRS]
    in_specs = [
        pl.BlockSpec((ROW_TILE, D_MODEL), row),
        pl.BlockSpec((ROW_TILE, D_A), row),
        *dil, *dil,
        pl.BlockSpec((ROW_TILE, D_M), row),
        pl.BlockSpec((ROW_TILE, N_BRANCH * D_MODEL), row),
        pl.BlockSpec((1, 1, N_MEM, 2 * D_M), lambda i: (layer, i // tiles_per_seq, 0, 0)),
        _layer_param((D_A, D_MODEL), layer), _layer_param((D_B_OUT, D_MODEL), layer),
        _layer_param((D_M, D_MODEL), layer), _layer_param((D_MODEL, D_MODEL), layer),
        _layer_param((1, D_MODEL), layer),
    ]
    return pl.pallas_call(
        _merge_kernel,
        grid=(t // ROW_TILE,),
        in_specs=in_specs,
        out_specs=pl.BlockSpec((ROW_TILE, D_MODEL), row),
        out_shape=jax.ShapeDtypeStruct((t, D_MODEL), F32),
        scratch_shapes=[pltpu.VMEM((2 * (_N_GROUPS - 1), ROW_TILE, LANES), F32)],
        compiler_params=_params(),
        name="merge",
    )(x2, oa, *ob, *lse, qm, gates, kv, wa, wb, wm, wo, g)


FF_CHUNK = 256


def _ffn_kernel(xm_ref, xp_ref, xn_ref, g1_ref, wup_ref, cw_ref, cb_ref, wdn_ref, g2_ref, out_ref,
                hext_ref, f_ref, *, tiles_per_seq):
    i = pl.program_id(0)
    pos = i % tiles_per_seq
    g1 = g1_ref[0]
    x = xm_ref[...]
    hext_ref[0:FF_TM, :] = _rms(x, g1).astype(BF16)
    hn = jnp.where(pos == tiles_per_seq - 1, 0.0, _rms(xn_ref[...], g1))
    hp = jnp.where(pos == 0, 0.0, _rms(xp_ref[...], g1))
    hext_ref[FF_TM:FF_TM + 2 * SUB, :] = jnp.concatenate([hn, hp], axis=0).astype(BF16)
    hext = hext_ref[...]
    ext = FF_TM + 2 * SUB

    def conv(u, col):
        cs = slice(col, col + FF_CHUNK)
        before = pltpu.roll(u, 1, 0)[0:FF_TM]
        after = pltpu.roll(u, ext - 1, 0)[0:FF_TM]
        y = cb_ref[0, :, cs] + before * cw_ref[0, 0:1, cs]
        y = y + u[0:FF_TM] * cw_ref[0, 1:2, cs]
        return y + after * cw_ref[0, 2:3, cs]

    for c in range(D_FF // FF_CHUNK):
        ca, cb = c * FF_CHUNK, D_FF + c * FF_CHUNK
        ua = conv(_dot(hext, wup_ref[0, :, ca:ca + FF_CHUNK]), ca)
        ub = conv(_dot(hext, wup_ref[0, :, cb:cb + FF_CHUNK]), cb)
        f_ref[:, ca:ca + FF_CHUNK] = (jax.nn.gelu(ua) * ub).astype(BF16)

    y = _dot(f_ref[...], wdn_ref[0])
    out_ref[...] = x + _rms(y, g2_ref[0])


def _ffn(x2, g1, w_up, cw, cb, w_dn, g2, layer, s):
    t = x2.shape[0]
    assert s % FF_TM == 0
    tiles_per_seq = s // FF_TM
    per = FF_TM // SUB
    nh = t // SUB
    row = lambda i: (i, 0)
    return pl.pallas_call(
        functools.partial(_ffn_kernel, tiles_per_seq=tiles_per_seq),
        grid=(t // FF_TM,),
        in_specs=[
            pl.BlockSpec((FF_TM, D_MODEL), row),
            pl.BlockSpec((SUB, D_MODEL), lambda i: (jnp.maximum(i * per - 1, 0), 0)),
            pl.BlockSpec((SUB, D_MODEL), lambda i: (jnp.minimum((i + 1) * per, nh - 1), 0)),
            _layer_param((1, D_MODEL), layer),
            _layer_param((D_MODEL, 2 * D_FF), layer),
            _layer_param((3, 2 * D_FF), layer),
            _layer_param((1, 2 * D_FF), layer),
            _layer_param((D_FF, D_MODEL), layer),
            _layer_param((1, D_MODEL), layer),
        ],
        out_specs=pl.BlockSpec((FF_TM, D_MODEL), row),
        out_shape=jax.ShapeDtypeStruct((t, D_MODEL), F32),
        scratch_shapes=[pltpu.VMEM((FF_TM + 2 * SUB, D_MODEL), BF16),
                        pltpu.VMEM((FF_TM, D_FF), BF16)],
        compiler_params=_params(),
        name="ffn",
    )(x2, x2, x2, g1, w_up, cw, cb, w_dn, g2)


def kernel(x, mem, mem_norm_g, g_pre_mix, w_in, rpb_na, w_mem_kv, b_gate, w_br_a, w_br_b, w_br_m,
           w_out, g_post_mix, g_pre_ffn, w_up, conv_w, conv_b, w_down, g_post_ffn):
    b, s, d = x.shape
    depth = w_in.shape[0]
    assert d == D_MODEL and s % GRID_W == 0 and s % ROW_TILE == 0
    rows = s // GRID_W
    assert rows >= NA_KH and rows % NA_ROWS_PER_STEP == 0

    bf = lambda a: a.astype(BF16)
    vec = lambda a: a.reshape(depth, 1, -1).astype(F32)

    w_in, w_up, w_down, w_out = bf(w_in), bf(w_up), bf(w_down), bf(w_out)
    w_br_a, w_br_b, w_br_m = bf(w_br_a), bf(w_br_b), bf(w_br_m)
    g_pre_mix, g_post_mix, g_pre_ffn, g_post_ffn = map(vec, (g_pre_mix, g_post_mix, g_pre_ffn,
                                                             g_post_ffn))
    b_gate, conv_b, conv_w = vec(b_gate), vec(conv_b), conv_w.astype(F32)
    assert rpb_na.shape[1:] == (NA_HEADS, 2 * NA_KH - 1, 2 * NA_KW - 1)
    rp = jnp.pad(rpb_na.astype(F32) * LOG2E, ((0, 0), (0, 0), (0, 0),
                                      (NA_RPB_PAD, LANES - NA_RPB_PAD - (2 * NA_KW - 1))))
    kv = _memkv(mem, mem_norm_g.reshape(1, -1).astype(F32), bf(w_mem_kv))

    x2 = x.reshape(b * s, d)
    for l in range(depth):
        qa, ka, va, b0, b1, b2, qm, gates = _inproj(x2, g_pre_mix, w_in, b_gate, l, b, s)
        oa = _na_attention(qa, ka, va, rp, l, b, s)
        ob, lse = zip(*(_dil_attention(q, grp) for grp, q in enumerate((b0, b1, b2))))
        x2 = _merge(x2, oa, ob, lse, qm, gates, kv, w_br_a, w_br_b, w_br_m, w_out, g_post_mix, l, s)
        x2 = _ffn(x2, g_pre_ffn, w_up, conv_w, conv_b, w_down, g_post_ffn, l, s)
    return x2.reshape(b, s, d)
```

```python
import functools

import jax
import jax.numpy as jnp
from jax import lax
from jax.experimental import pallas as pl
from jax.experimental.pallas import tpu as pltpu

D_MODEL = 1024
GRID_W = 64
N_MEM = 256
HEAD_DIM = 64
NA_HEADS = 6
NA_KH = 8
NA_KW = 16
DIL_PAIRS = ((128, 1), (512, 4), (2048, 16))
DIL_HEADS = 6
MEM_HEADS = 4
D_A = 384
D_B = 384
D_B_OUT = 128
D_M = 256
N_BRANCH = 3
D_IN = 3 * D_A + 3 * D_B + D_M + N_BRANCH * D_MODEL
D_FF = 2816
RMS_EPS = 1e-6
NEG_INF = -1e30

LANES = 128
SUB = 8
HALO = 16
VMEM_LIMIT = 56 * 1024 * 1024
IN_TM = 1024
ROW_TILE = 1024
FF_TM = 1024

F32 = jnp.float32
BF16 = jnp.bfloat16

LOG2E = 1.4426950408889634
LN2 = 0.6931471805599453
Q_SCALE = HEAD_DIM ** -0.5 * LOG2E


def _rms(x, g):
    return x * lax.rsqrt(jnp.mean(x * x, axis=-1, keepdims=True) + RMS_EPS) * g


def _dot(a, b):
    return jnp.dot(a, b, preferred_element_type=F32)


def _dot_nt(a, b):
    return lax.dot_general(a, b, (((1,), (1,)), ((), ())), preferred_element_type=F32)


def _layer_param(shape, layer):
    zeros = (0,) * len(shape)
    return pl.BlockSpec((1,) + tuple(shape), lambda *_: (layer,) + zeros,
                        pipeline_mode=pl.Buffered(1))


def _params(n_axes=1):
    return pltpu.CompilerParams(dimension_semantics=("arbitrary",) * n_axes,
                                vmem_limit_bytes=VMEM_LIMIT)


def _memkv_kernel(mem_ref, g_ref, w_ref, o_ref):
    mn = _rms(mem_ref[0], g_ref[...]).astype(BF16)
    o_ref[0, 0] = _dot(mn, w_ref[0]).astype(BF16)


def _memkv(mem, g, w_kv):
    depth = w_kv.shape[0]
    b = mem.shape[0]
    return pl.pallas_call(
        _memkv_kernel,
        grid=(depth, b),
        in_specs=[
            pl.BlockSpec((1, N_MEM, D_MODEL), lambda l, i: (i, 0, 0)),
            pl.BlockSpec((1, D_MODEL), lambda l, i: (0, 0)),
            pl.BlockSpec((1, D_MODEL, 2 * D_M), lambda l, i: (l, 0, 0)),
        ],
        out_specs=pl.BlockSpec((1, 1, N_MEM, 2 * D_M), lambda l, i: (l, i, 0, 0)),
        out_shape=jax.ShapeDtypeStruct((depth, b, N_MEM, 2 * D_M), BF16),
        compiler_params=_params(2),
        name="memkv",
    )(mem, g, w_kv)


IN_CHUNK = 512
_A_SLABS = D_A // LANES
_M_SLABS = D_M // LANES
_N_GROUPS = len(DIL_PAIRS)


def _inproj_kernel(x_ref, g_ref, w_ref, bg_ref, qa_ref, ka_ref, va_ref, b0_ref, b1_ref, b2_ref,
                   qm_ref, gate_ref, stage_ref):
    h = _rms(x_ref[...], g_ref[0]).astype(BF16)
    scale = Q_SCALE
    b_refs = (b0_ref, b1_ref, b2_ref)

    def put_dilated(which, grp, val):
        d = DIL_PAIRS[grp][1]
        cs = slice(which * LANES, (which + 1) * LANES)
        if d == 1:
            b_refs[grp][0, 0, :, cs] = val.astype(BF16)
            return
        stage = stage_ref.at[(grp - 1) * 3 + which]
        stage[...] = val
        for r in range(d):
            b_refs[grp][0, r, :, cs] = stage[pl.ds(r, IN_TM // d, stride=d), :].astype(BF16)

    def put(slab, val):
        s = slab
        if s < 3 * _A_SLABS:
            which, j = divmod(s, _A_SLABS)
            ref = (qa_ref, ka_ref, va_ref)[which]
            if which == 0:
                val = val * scale
            ref[:, j * LANES:(j + 1) * LANES] = val.astype(BF16)
            return
        s -= 3 * _A_SLABS
        if s < 3 * _N_GROUPS:
            which, grp = divmod(s, _N_GROUPS)
            put_dilated(which, grp, val * scale if which == 0 else val)
            return
        s -= 3 * _N_GROUPS
        if s < _M_SLABS:
            qm_ref[:, s * LANES:(s + 1) * LANES] = (val * scale).astype(BF16)
            return
        s -= _M_SLABS
        gate = jax.nn.sigmoid(val + bg_ref[0, :, s * LANES:(s + 1) * LANES])
        gate_ref[:, s * LANES:(s + 1) * LANES] = gate.astype(BF16)

    per = IN_CHUNK // LANES
    for c in range(D_IN // IN_CHUNK):
        r = _dot(h, w_ref[0, :, c * IN_CHUNK:(c + 1) * IN_CHUNK])
        for j in range(per):
            put(c * per + j, r[:, j * LANES:(j + 1) * LANES])


def _inproj(x2, g, w_in, b_gate, layer, b, s):
    t = x2.shape[0]
    assert s % IN_TM == 0
    tiles_per_seq = s // IN_TM
    row = lambda i: (i, 0)
    outs = [jax.ShapeDtypeStruct((t, D_A), BF16)] * 3
    out_specs = [pl.BlockSpec((IN_TM, D_A), row)] * 3
    for _, d in DIL_PAIRS:
        assert IN_TM % (d * HALO) == 0
        outs.append(jax.ShapeDtypeStruct((b, d, s // d, 3 * LANES), BF16))
        out_specs.append(pl.BlockSpec((1, d, IN_TM // d, 3 * LANES),
                                      lambda i: (i // tiles_per_seq, 0, i % tiles_per_seq, 0)))
    outs += [jax.ShapeDtypeStruct((t, D_M), BF16),
             jax.ShapeDtypeStruct((t, N_BRANCH * D_MODEL), BF16)]
    out_specs += [pl.BlockSpec((IN_TM, D_M), row),
                  pl.BlockSpec((IN_TM, N_BRANCH * D_MODEL), row)]
    return pl.pallas_call(
        _inproj_kernel,
        grid=(t // IN_TM,),
        in_specs=[
            pl.BlockSpec((IN_TM, D_MODEL), row),
            _layer_param((1, D_MODEL), layer),
            _layer_param((D_MODEL, D_IN), layer),
            _layer_param((1, N_BRANCH * D_MODEL), layer),
        ],
        out_specs=out_specs,
        out_shape=outs,
        scratch_shapes=[pltpu.VMEM((3 * (_N_GROUPS - 1), IN_TM, LANES), F32)],
        compiler_params=_params(),
        name="inproj",
    )(x2, g, w_in, b_gate)


def _pair_scores(q2, k2):
    lane = lax.broadcasted_iota(jnp.int32, q2.shape, 1)
    zero = jnp.zeros_like(q2)
    qs = jnp.concatenate([jnp.where(lane < HEAD_DIM, q2, zero),
                          jnp.where(lane < HEAD_DIM, zero, q2)], axis=0)
    return _dot_nt(qs, k2)


def _softmax_parts(s):
    m = jnp.max(s, axis=-1, keepdims=True)
    p = jnp.exp2(s - m)
    return m, p.astype(BF16), jnp.sum(p, axis=-1, keepdims=True)


def _unstack_heads(a):
    m_rows = a.shape[0] // 2
    lane = lax.broadcasted_iota(jnp.int32, (m_rows, LANES), 1)
    return jnp.where(lane < HEAD_DIM, a[:m_rows], a[m_rows:])


NA_ROWS_PER_STEP = 8
NA_GROUP = 4
NA_BAND = NA_KH * GRID_W


NA_RPB_PAD = GRID_W - NA_KW


def _na_build_bias(rp_ref, toe_ref, tab_ref):
    lane = lax.broadcasted_iota(jnp.int32, (GRID_W, LANES), 1)
    q = lax.broadcasted_iota(jnp.int32, (GRID_W, LANES), 0)
    c = lane % GRID_W
    c0 = jnp.clip(q - NA_KW // 2, 0, GRID_W - NA_KW)
    valid = (c >= c0) & (c < c0 + NA_KW)
    base_lo, base_hi = LANES - (GRID_W - 1), LANES - (GRID_W - 1) - GRID_W

    def toe_body(ro, carry):
        for h in range(NA_HEADS):
            row = jnp.broadcast_to(rp_ref[0, h, pl.ds(ro, 1), :], (GRID_W, LANES))
            lo = pltpu.roll(row, base_lo, 1, stride=1, stride_axis=0)
            hi = pltpu.roll(row, base_hi, 1, stride=1, stride_axis=0)
            toe_ref[h, ro] = jnp.where(valid, jnp.where(lane < GRID_W, lo, hi), NEG_INF)
        return carry

    lax.fori_loop(0, 2 * NA_KH - 1, toe_body, 0)

    def tab_body(dl, carry):
        for h in range(NA_HEADS):
            rows_ = slice((h % 2) * GRID_W, (h % 2 + 1) * GRID_W)
            for g in range(NA_KH // 2):
                even, odd = toe_ref[h, dl + 2 * g], toe_ref[h, dl + 2 * g + 1]
                tab_ref[h // 2, dl, rows_, g * LANES:(g + 1) * LANES] = \
                    jnp.where(lane < GRID_W, even, odd)
        return carry

    lax.fori_loop(0, NA_KH, tab_body, 0)


def _na_kernel(q_ref, k_ref, v_ref, rp_ref, o_ref, toe_ref, tab_ref, *, rows):
    rb = pl.program_id(1)

    @pl.when((pl.program_id(0) == 0) & (rb == 0))
    def _():
        _na_build_bias(rp_ref, toe_ref, tab_ref)

    def group_body(gi, carry):
        chains = []
        for lg in range(NA_GROUP):
            lr = gi * NA_GROUP + lg
            i = rb * NA_ROWS_PER_STEP + lr
            r0 = jnp.clip(i - NA_KH // 2, 0, rows - NA_KH)
            ks = pl.multiple_of(r0 * GRID_W, GRID_W)
            qs = pl.multiple_of(lr * GRID_W, GRID_W)
            for hp in range(NA_HEADS // 2):
                chains.append((qs, ks, r0 - i + (NA_KH - 1), hp,
                               slice(hp * LANES, (hp + 1) * LANES)))
        ss = [_pair_scores(q_ref[0, pl.ds(qs, GRID_W), cs], k_ref[0, pl.ds(ks, NA_BAND), cs])
              for qs, ks, dl, hp, cs in chains]
        parts = [_softmax_parts(s + tab_ref[hp, dl])
                 for s, (qs, ks, dl, hp, cs) in zip(ss, chains)]
        for (_, p, den), (qs, ks, dl, hp, cs) in zip(parts, chains):
            pv = _dot(p, v_ref[0, pl.ds(ks, NA_BAND), cs])
            o_ref[0, pl.ds(qs, GRID_W), cs] = _unstack_heads(pv / den).astype(BF16)
        return carry

    lax.fori_loop(0, NA_ROWS_PER_STEP // NA_GROUP, group_body, 0)


def _na_attention(q, k, v, rp, layer, b, s):
    rows = s // GRID_W
    qblk = NA_ROWS_PER_STEP * GRID_W
    q3, k3, v3 = (a.reshape(b, s, D_A) for a in (q, k, v))
    out = pl.pallas_call(
        functools.partial(_na_kernel, rows=rows),
        grid=(b, rows // NA_ROWS_PER_STEP),
        in_specs=[
            pl.BlockSpec((1, qblk, D_A), lambda i, j: (i, j, 0)),
            pl.BlockSpec((1, s, D_A), lambda i, j: (i, 0, 0)),
            pl.BlockSpec((1, s, D_A), lambda i, j: (i, 0, 0)),
            _layer_param(rp.shape[1:], layer),
        ],
        out_specs=pl.BlockSpec((1, qblk, D_A), lambda i, j: (i, j, 0)),
        out_shape=jax.ShapeDtypeStruct((b, s, D_A), BF16),
        scratch_shapes=[pltpu.VMEM((NA_HEADS, 2 * NA_KH - 1, GRID_W, LANES), F32),
                        pltpu.VMEM((NA_HEADS // 2, NA_KH, 2 * GRID_W, NA_BAND), F32)],
        compiler_params=_params(2),
        name="na_attn",
    )(q3, k3, v3, rp)
    return out.reshape(b * s, D_A)


DIL_QB = 128
DIL_HALF = 64
DIL_KB = DIL_QB + 2 * DIL_HALF
DIL_GROUP = 8
DIL_KEY_SHIFTS = (0, -1, -2)


def _alibi_slope(head):
    return 2.0 ** (-8.0 * (head + 1) / DIL_HEADS)


def _dil_kernel(x_ref, o_ref, lse_ref, bias_ref, *, length, dilation, slopes, residues):
    nblk = length // DIL_QB

    @pl.when((pl.program_id(0) == 0) & (pl.program_id(1) == 0))
    def _():
        rel0 = (lax.broadcasted_iota(jnp.int32, (DIL_QB, DIL_KB), 1)
                - lax.broadcasted_iota(jnp.int32, (DIL_QB, DIL_KB), 0))
        for c, shift in enumerate(DIL_KEY_SHIFTS):
            dist = jnp.abs(rel0 + shift * DIL_HALF)
            dist_f = (dist * dilation).astype(F32)
            for hh in range(2):
                bias = jnp.where(dist <= DIL_HALF, (-slopes[hh] * LOG2E) * dist_f, NEG_INF)
                bias_ref[c, hh * DIL_QB:(hh + 1) * DIL_QB, :] = bias

    def group(chains):
        geo = []
        for r, bi in chains:
            q0 = bi * DIL_QB
            if isinstance(bi, int):
                ks = min(max(q0 - DIL_HALF, 0), length - DIL_KB)
            else:
                q0 = pl.multiple_of(q0, DIL_QB)
                ks = pl.multiple_of(jnp.clip(q0 - DIL_HALF, 0, length - DIL_KB), DIL_HALF)
            geo.append((r, q0, ks, (q0 - ks) // DIL_HALF))
        ss = [_pair_scores(x_ref[0, r, pl.ds(q0, DIL_QB), 0:LANES],
                           x_ref[0, r, pl.ds(ks, DIL_KB), LANES:2 * LANES])
              for r, q0, ks, case in geo]
        parts = [_softmax_parts(s + bias_ref[case]) for s, (r, q0, ks, case) in zip(ss, geo)]
        for (m, p, den), (r, q0, ks, case) in zip(parts, geo):
            pv = _dot(p, x_ref[0, r, pl.ds(ks, DIL_KB), 2 * LANES:3 * LANES])
            o_ref[0, r, pl.ds(q0, DIL_QB), :] = _unstack_heads(pv / den).astype(BF16)
            lse = jnp.broadcast_to(m * LN2 + jnp.log(den), (2 * DIL_QB, LANES))
            lse_ref[0, r, pl.ds(q0, DIL_QB), :] = _unstack_heads(lse)

    if residues * nblk <= DIL_GROUP:
        group([(r, bi) for r in range(residues) for bi in range(nblk)])
    else:
        assert nblk % DIL_GROUP == 0
        for r in range(residues):
            def body(gi, carry, r=r):
                group([(r, gi * DIL_GROUP + j) for j in range(DIL_GROUP)])
                return carry
            lax.fori_loop(0, nblk // DIL_GROUP, body, 0)


def _dil_attention(qkv, grp):
    window, dilation = DIL_PAIRS[grp]
    assert window // 2 // dilation == DIL_HALF
    b, _, length, width = qkv.shape
    assert length % DIL_QB == 0 and length >= DIL_KB
    nblk = length // DIL_QB
    residues = min(dilation, max(1, DIL_GROUP // nblk))
    assert dilation % residues == 0
    slopes = tuple(_alibi_slope(2 * grp + hh) for hh in range(2))
    blk = lambda w: pl.BlockSpec((1, residues, length, w), lambda i, r: (i, r, 0, 0))
    return pl.pallas_call(
        functools.partial(_dil_kernel, length=length, dilation=dilation, slopes=slopes,
                          residues=residues),
        grid=(b, dilation // residues),
        in_specs=[blk(width)],
        out_specs=[blk(LANES), blk(LANES)],
        out_shape=[jax.ShapeDtypeStruct((b, dilation, length, LANES), BF16),
                   jax.ShapeDtypeStruct((b, dilation, length, LANES), F32)],
        scratch_shapes=[pltpu.VMEM((len(DIL_KEY_SHIFTS), 2 * DIL_QB, DIL_KB), F32)],
        compiler_params=_params(2),
        name=f"dil_attn_{grp}",
    )(qkv)


def _merge_kernel(x_ref, oa_ref, o0_ref, o1_ref, o2_ref, l0_ref, l1_ref, l2_ref, qm_ref, gate_ref,
                  kv_ref, wa_ref, wb_ref, wm_ref, wo_ref, g_ref, out_ref, stage_ref):
    pairs = range(MEM_HEADS // 2)
    ss = [_pair_scores(qm_ref[:, hp * LANES:(hp + 1) * LANES],
                       kv_ref[0, 0, :, hp * LANES:(hp + 1) * LANES]) for hp in pairs]
    proj_a = _dot(oa_ref[...], wa_ref[0])

    def natural_order(ref, grp, slot):
        d = DIL_PAIRS[grp][1]
        if d == 1:
            return ref[0, 0].astype(F32)
        stage = stage_ref.at[slot]
        for r in range(d):
            stage[pl.ds(r, ROW_TILE // d, stride=d), :] = ref[0, r].astype(F32)
        return stage[...]

    o_refs, l_refs = (o0_ref, o1_ref, o2_ref), (l0_ref, l1_ref, l2_ref)
    os_ = [natural_order(o_refs[g], g, 2 * (g - 1)) for g in range(_N_GROUPS)]
    ls = [natural_order(l_refs[g], g, 2 * (g - 1) + 1) for g in range(_N_GROUPS)]
    mx = jnp.maximum(jnp.maximum(ls[0], ls[1]), ls[2])
    es = [jnp.exp(l - mx) for l in ls]
    inv = 1.0 / (es[0] + es[1] + es[2])
    o_b = ((es[0] * inv) * os_[0] + (es[1] * inv) * os_[1] + (es[2] * inv) * os_[2]).astype(BF16)
    proj_b = _dot(o_b, wb_ref[0])

    om = []
    for hp, s in zip(pairs, ss):
        _, p, den = _softmax_parts(s)
        pv = _dot(p, kv_ref[0, 0, :, D_M + hp * LANES:D_M + (hp + 1) * LANES])
        om.append(_unstack_heads(pv / den).astype(BF16))
    o_m = jnp.concatenate(om, axis=-1)

    merged = (gate_ref[:, 0:D_MODEL].astype(F32) * proj_a
              + gate_ref[:, D_MODEL:2 * D_MODEL].astype(F32) * proj_b
              + gate_ref[:, 2 * D_MODEL:3 * D_MODEL].astype(F32) * _dot(o_m, wm_ref[0]))
    merged = merged.astype(BF16)
    for r0 in range(0, ROW_TILE, FF_OUT_ROWS):
        rows = slice(r0, r0 + FF_OUT_ROWS)
        y = _dot(merged[rows], wo_ref[0])
        out_ref[rows, :] = x_ref[rows, :] + _rms(y, g_ref[0])


def _merge(x2, oa, ob, lse, qm, gates, kv, wa, wb, wm, wo, g, layer, s):
    t = x2.shape[0]
    tiles_per_seq = s // ROW_TILE
    row = lambda i: (i, 0)
    dil = [pl.BlockSpec((1, d, ROW_TILE // d, LANES),
                        lambda i: (i // tiles_per_seq, 0, i % tiles_per_seq, 0))
           for _, d in DIL_PAIRS]
    in_specs = [
        pl.BlockSpec((ROW_TILE, D_MODEL), row),
        pl.BlockSpec((ROW_TILE, D_A), row),
        *dil, *dil,
        pl.BlockSpec((ROW_TILE, D_M), row),
        pl.BlockSpec((ROW_TILE, N_BRANCH * D_MODEL), row),
        pl.BlockSpec((1, 1, N_MEM, 2 * D_M), lambda i: (layer, i // tiles_per_seq, 0, 0)),
        _layer_param((D_A, D_MODEL), layer), _layer_param((D_B_OUT, D_MODEL), layer),
        _layer_param((D_M, D_MODEL), layer), _layer_param((D_MODEL, D_MODEL), layer),
        _layer_param((1, D_MODEL), layer),
    ]
    return pl.pallas_call(
        _merge_kernel,
        grid=(t // ROW_TILE,),
        in_specs=in_specs,
        out_specs=pl.BlockSpec((ROW_TILE, D_MODEL), row),
        out_shape=jax.ShapeDtypeStruct((t, D_MODEL), F32),
        scratch_shapes=[pltpu.VMEM((2 * (_N_GROUPS - 1), ROW_TILE, LANES), F32)],
        compiler_params=_params(),
        name="merge",
    )(x2, oa, *ob, *lse, qm, gates, kv, wa, wb, wm, wo, g)


FF_CHUNK = 256
FF_OUT_ROWS = 256


def _ffn_kernel(xm_ref, xp_ref, xn_ref, g1_ref, wup_ref, cw_ref, cb_ref, wdn_ref, g2_ref, out_ref,
                hext_ref, f_ref, *, tiles_per_seq):
    i = pl.program_id(0)
    pos = i % tiles_per_seq
    g1 = g1_ref[0]
    x = xm_ref[...]
    hext_ref[0:FF_TM, :] = _rms(x, g1).astype(BF16)
    hn = jnp.where(pos == tiles_per_seq - 1, 0.0, _rms(xn_ref[...], g1))
    hp = jnp.where(pos == 0, 0.0, _rms(xp_ref[...], g1))
    hext_ref[FF_TM:FF_TM + 2 * SUB, :] = jnp.concatenate([hn, hp], axis=0).astype(BF16)
    hext = hext_ref[...]
    ext = FF_TM + 2 * SUB

    def conv(u, col):
        cs = slice(col, col + FF_CHUNK)
        before = pltpu.roll(u, 1, 0)[0:FF_TM]
        after = pltpu.roll(u, ext - 1, 0)[0:FF_TM]
        y = cb_ref[0, :, cs] + before * cw_ref[0, 0:1, cs]
        y = y + u[0:FF_TM] * cw_ref[0, 1:2, cs]
        return y + after * cw_ref[0, 2:3, cs]

    for c in range(D_FF // FF_CHUNK):
        ca, cb = c * FF_CHUNK, D_FF + c * FF_CHUNK
        ua = conv(_dot(hext, wup_ref[0, :, ca:ca + FF_CHUNK]), ca)
        ub = conv(_dot(hext, wup_ref[0, :, cb:cb + FF_CHUNK]), cb)
        f_ref[:, ca:ca + FF_CHUNK] = (jax.nn.gelu(ua) * ub).astype(BF16)

    for r0 in range(0, FF_TM, FF_OUT_ROWS):
        rows = slice(r0, r0 + FF_OUT_ROWS)
        y = _dot(f_ref[rows, :], wdn_ref[0])
        out_ref[rows, :] = xm_ref[rows, :] + _rms(y, g2_ref[0])


def _ffn(x2, g1, w_up, cw, cb, w_dn, g2, layer, s):
    t = x2.shape[0]
    assert s % FF_TM == 0
    tiles_per_seq = s // FF_TM
    per = FF_TM // SUB
    nh = t // SUB
    row = lambda i: (i, 0)
    return pl.pallas_call(
        functools.partial(_ffn_kernel, tiles_per_seq=tiles_per_seq),
        grid=(t // FF_TM,),
        in_specs=[
            pl.BlockSpec((FF_TM, D_MODEL), row),
            pl.BlockSpec((SUB, D_MODEL), lambda i: (jnp.maximum(i * per - 1, 0), 0)),
            pl.BlockSpec((SUB, D_MODEL), lambda i: (jnp.minimum((i + 1) * per, nh - 1), 0)),
            _layer_param((1, D_MODEL), layer),
            _layer_param((D_MODEL, 2 * D_FF), layer),
            _layer_param((3, 2 * D_FF), layer),
            _layer_param((1, 2 * D_FF), layer),
            _layer_param((D_FF, D_MODEL), layer),
            _layer_param((1, D_MODEL), layer),
        ],
        out_specs=pl.BlockSpec((FF_TM, D_MODEL), row),
        out_shape=jax.ShapeDtypeStruct((t, D_MODEL), F32),
        scratch_shapes=[pltpu.VMEM((FF_TM + 2 * SUB, D_MODEL), BF16),
                        pltpu.VMEM((FF_TM, D_FF), BF16)],
        compiler_params=_params(),
        name="ffn",
    )(x2, x2, x2, g1, w_up, cw, cb, w_dn, g2)


def kernel(x, mem, mem_norm_g, g_pre_mix, w_in, rpb_na, w_mem_kv, b_gate, w_br_a, w_br_b, w_br_m,
           w_out, g_post_mix, g_pre_ffn, w_up, conv_w, conv_b, w_down, g_post_ffn):
    b, s, d = x.shape
    depth = w_in.shape[0]
    assert d == D_MODEL and s % GRID_W == 0 and s % ROW_TILE == 0
    rows = s // GRID_W
    assert rows >= NA_KH and rows % NA_ROWS_PER_STEP == 0

    bf = lambda a: a.astype(BF16)
    vec = lambda a: a.reshape(depth, 1, -1).astype(F32)

    w_in, w_up, w_down, w_out = bf(w_in), bf(w_up), bf(w_down), bf(w_out)
    w_br_a, w_br_b, w_br_m = bf(w_br_a), bf(w_br_b), bf(w_br_m)
    g_pre_mix, g_post_mix, g_pre_ffn, g_post_ffn = map(vec, (g_pre_mix, g_post_mix, g_pre_ffn,
                                                             g_post_ffn))
    b_gate, conv_b, conv_w = vec(b_gate), vec(conv_b), conv_w.astype(F32)
    assert rpb_na.shape[1:] == (NA_HEADS, 2 * NA_KH - 1, 2 * NA_KW - 1)
    rp = jnp.pad(rpb_na.astype(F32) * LOG2E, ((0, 0), (0, 0), (0, 0),
                                      (NA_RPB_PAD, LANES - NA_RPB_PAD - (2 * NA_KW - 1))))
    kv = _memkv(mem, mem_norm_g.reshape(1, -1).astype(F32), bf(w_mem_kv))

    x2 = x.reshape(b * s, d)
    for l in range(depth):
        qa, ka, va, b0, b1, b2, qm, gates = _inproj(x2, g_pre_mix, w_in, b_gate, l, b, s)
        oa = _na_attention(qa, ka, va, rp, l, b, s)
        ob, lse = zip(*(_dil_attention(q, grp) for grp, q in enumerate((b0, b1, b2))))
        x2 = _merge(x2, oa, ob, lse, qm, gates, kv, w_br_a, w_br_b, w_br_m, w_out, g_post_mix, l, s)
        x2 = _ffn(x2, g_pre_ffn, w_up, conv_w, conv_b, w_down, g_post_ffn, l, s)
    return x2.reshape(b, s, d)
```

```python
import functools

import jax
import jax.numpy as jnp
from jax import lax
from jax.experimental import pallas as pl
from jax.experimental.pallas import tpu as pltpu

D_MODEL = 1024
GRID_W = 64
N_MEM = 256
HEAD_DIM = 64
NA_HEADS = 6
NA_KH = 8
NA_KW = 16
DIL_PAIRS = ((128, 1), (512, 4), (2048, 16))
DIL_HEADS = 6
MEM_HEADS = 4
D_A = 384
D_B = 384
D_B_OUT = 128
D_M = 256
N_BRANCH = 3
D_IN = 3 * D_A + 3 * D_B + D_M + N_BRANCH * D_MODEL
D_FF = 2816
RMS_EPS = 1e-6
NEG_INF = -1e30

LANES = 128
SUB = 8
HALO = 16
VMEM_LIMIT = 56 * 1024 * 1024
IN_TM = 1024
ROW_TILE = 1024
FF_TM = 1024

F32 = jnp.float32
BF16 = jnp.bfloat16

LOG2E = 1.4426950408889634
LN2 = 0.6931471805599453
Q_SCALE = HEAD_DIM ** -0.5 * LOG2E


def _rms(x, g):
    return x * lax.rsqrt(jnp.mean(x * x, axis=-1, keepdims=True) + RMS_EPS) * g


def _dot(a, b):
    return jnp.dot(a, b, preferred_element_type=F32)


def _dot_nt(a, b):
    return lax.dot_general(a, b, (((1,), (1,)), ((), ())), preferred_element_type=F32)


def _layer_param(shape, layer):
    zeros = (0,) * len(shape)
    return pl.BlockSpec((1,) + tuple(shape), lambda *_: (layer,) + zeros,
                        pipeline_mode=pl.Buffered(1))


def _params(n_axes=1):
    return pltpu.CompilerParams(dimension_semantics=("arbitrary",) * n_axes,
                                vmem_limit_bytes=VMEM_LIMIT)


def _memkv_kernel(mem_ref, g_ref, w_ref, o_ref):
    mn = _rms(mem_ref[0], g_ref[...]).astype(BF16)
    o_ref[0, 0] = _dot(mn, w_ref[0]).astype(BF16)


def _memkv(mem, g, w_kv):
    depth = w_kv.shape[0]
    b = mem.shape[0]
    return pl.pallas_call(
        _memkv_kernel,
        grid=(depth, b),
        in_specs=[
            pl.BlockSpec((1, N_MEM, D_MODEL), lambda l, i: (i, 0, 0)),
            pl.BlockSpec((1, D_MODEL), lambda l, i: (0, 0)),
            pl.BlockSpec((1, D_MODEL, 2 * D_M), lambda l, i: (l, 0, 0)),
        ],
        out_specs=pl.BlockSpec((1, 1, N_MEM, 2 * D_M), lambda l, i: (l, i, 0, 0)),
        out_shape=jax.ShapeDtypeStruct((depth, b, N_MEM, 2 * D_M), BF16),
        compiler_params=_params(2),
        name="memkv",
    )(mem, g, w_kv)


IN_CHUNK = 512
_A_SLABS = D_A // LANES
_M_SLABS = D_M // LANES
_N_GROUPS = len(DIL_PAIRS)


def _inproj_kernel(x_ref, g_ref, w_ref, bg_ref, qa_ref, ka_ref, va_ref, b0_ref, b1_ref, b2_ref,
                   qm_ref, gate_ref, stage_ref):
    x = x_ref[...]
    h = (x * g_ref[0]).astype(BF16)
    inv = jnp.broadcast_to(lax.rsqrt(jnp.mean(x * x, axis=-1, keepdims=True) + RMS_EPS),
                           (IN_TM, LANES))
    inv_q = inv * Q_SCALE
    b_refs = (b0_ref, b1_ref, b2_ref)

    def put_dilated(which, grp, val):
        d = DIL_PAIRS[grp][1]
        cs = slice(which * LANES, (which + 1) * LANES)
        if d == 1:
            b_refs[grp][0, 0, :, cs] = val.astype(BF16)
            return
        stage = stage_ref.at[(grp - 1) * 3 + which]
        stage[...] = val
        for r in range(d):
            b_refs[grp][0, r, :, cs] = stage[pl.ds(r, IN_TM // d, stride=d), :].astype(BF16)

    def put(slab, val):
        s = slab
        if s < 3 * _A_SLABS:
            which, j = divmod(s, _A_SLABS)
            ref = (qa_ref, ka_ref, va_ref)[which]
            ref[:, j * LANES:(j + 1) * LANES] = (val * (inv_q if which == 0 else inv)).astype(BF16)
            return
        s -= 3 * _A_SLABS
        if s < 3 * _N_GROUPS:
            which, grp = divmod(s, _N_GROUPS)
            put_dilated(which, grp, val * (inv_q if which == 0 else inv))
            return
        s -= 3 * _N_GROUPS
        if s < _M_SLABS:
            qm_ref[:, s * LANES:(s + 1) * LANES] = (val * inv_q).astype(BF16)
            return
        s -= _M_SLABS
        gate = jax.nn.sigmoid(val * inv + bg_ref[0, :, s * LANES:(s + 1) * LANES])
        gate_ref[:, s * LANES:(s + 1) * LANES] = gate.astype(BF16)

    per = IN_CHUNK // LANES
    for c in range(D_IN // IN_CHUNK):
        r = _dot(h, w_ref[0, :, c * IN_CHUNK:(c + 1) * IN_CHUNK])
        for j in range(per):
            put(c * per + j, r[:, j * LANES:(j + 1) * LANES])


def _inproj(x2, g, w_in, b_gate, layer, b, s):
    t = x2.shape[0]
    assert s % IN_TM == 0
    tiles_per_seq = s // IN_TM
    row = lambda i: (i, 0)
    outs = [jax.ShapeDtypeStruct((t, D_A), BF16)] * 3
    out_specs = [pl.BlockSpec((IN_TM, D_A), row)] * 3
    for _, d in DIL_PAIRS:
        assert IN_TM % (d * HALO) == 0
        outs.append(jax.ShapeDtypeStruct((b, d, s // d, 3 * LANES), BF16))
        out_specs.append(pl.BlockSpec((1, d, IN_TM // d, 3 * LANES),
                                      lambda i: (i // tiles_per_seq, 0, i % tiles_per_seq, 0)))
    outs += [jax.ShapeDtypeStruct((t, D_M), BF16),
             jax.ShapeDtypeStruct((t, N_BRANCH * D_MODEL), BF16)]
    out_specs += [pl.BlockSpec((IN_TM, D_M), row),
                  pl.BlockSpec((IN_TM, N_BRANCH * D_MODEL), row)]
    return pl.pallas_call(
        _inproj_kernel,
        grid=(t // IN_TM,),
        in_specs=[
            pl.BlockSpec((IN_TM, D_MODEL), row),
            _layer_param((1, D_MODEL), layer),
            _layer_param((D_MODEL, D_IN), layer),
            _layer_param((1, N_BRANCH * D_MODEL), layer),
        ],
        out_specs=out_specs,
        out_shape=outs,
        scratch_shapes=[pltpu.VMEM((3 * (_N_GROUPS - 1), IN_TM, LANES), F32)],
        compiler_params=_params(),
        name="inproj",
    )(x2, g, w_in, b_gate)


def _pair_scores(q2, k2):
    lane = lax.broadcasted_iota(jnp.int32, q2.shape, 1)
    zero = jnp.zeros_like(q2)
    qs = jnp.concatenate([jnp.where(lane < HEAD_DIM, q2, zero),
                          jnp.where(lane < HEAD_DIM, zero, q2)], axis=0)
    return _dot_nt(qs, k2)


def _softmax_parts(s):
    m = jnp.max(s, axis=-1, keepdims=True)
    return m, jnp.exp2(s - m).astype(BF16)


def _pv_and_den(p, v2):
    r = _dot(p, jnp.concatenate([v2, jnp.ones_like(v2)], axis=1))
    return r[:, :LANES], r[:, LANES:]


def _unstack_heads(a):
    m_rows = a.shape[0] // 2
    lane = lax.broadcasted_iota(jnp.int32, (m_rows, LANES), 1)
    return jnp.where(lane < HEAD_DIM, a[:m_rows], a[m_rows:])


NA_ROWS_PER_STEP = 8
NA_GROUP = 4
NA_BAND = NA_KH * GRID_W


NA_RPB_PAD = GRID_W - NA_KW


def _na_build_bias(rp_ref, toe_ref, tab_ref):
    lane = lax.broadcasted_iota(jnp.int32, (GRID_W, LANES), 1)
    q = lax.broadcasted_iota(jnp.int32, (GRID_W, LANES), 0)
    c = lane % GRID_W
    c0 = jnp.clip(q - NA_KW // 2, 0, GRID_W - NA_KW)
    valid = (c >= c0) & (c < c0 + NA_KW)
    base_lo, base_hi = LANES - (GRID_W - 1), LANES - (GRID_W - 1) - GRID_W

    def toe_body(ro, carry):
        for h in range(NA_HEADS):
            row = jnp.broadcast_to(rp_ref[0, h, pl.ds(ro, 1), :], (GRID_W, LANES))
            lo = pltpu.roll(row, base_lo, 1, stride=1, stride_axis=0)
            hi = pltpu.roll(row, base_hi, 1, stride=1, stride_axis=0)
            toe_ref[h, ro] = jnp.where(valid, jnp.where(lane < GRID_W, lo, hi), NEG_INF)
        return carry

    lax.fori_loop(0, 2 * NA_KH - 1, toe_body, 0)

    def tab_body(dl, carry):
        for h in range(NA_HEADS):
            rows_ = slice((h % 2) * GRID_W, (h % 2 + 1) * GRID_W)
            for g in range(NA_KH // 2):
                even, odd = toe_ref[h, dl + 2 * g], toe_ref[h, dl + 2 * g + 1]
                tab_ref[h // 2, dl, rows_, g * LANES:(g + 1) * LANES] = \
                    jnp.where(lane < GRID_W, even, odd)
        return carry

    lax.fori_loop(0, NA_KH, tab_body, 0)


def _na_kernel(q_ref, k_ref, v_ref, rp_ref, o_ref, toe_ref, tab_ref, *, rows):
    rb = pl.program_id(1)

    @pl.when((pl.program_id(0) == 0) & (rb == 0))
    def _():
        _na_build_bias(rp_ref, toe_ref, tab_ref)

    def group_body(gi, carry):
        chains = []
        for lg in range(NA_GROUP):
            lr = gi * NA_GROUP + lg
            i = rb * NA_ROWS_PER_STEP + lr
            r0 = jnp.clip(i - NA_KH // 2, 0, rows - NA_KH)
            ks = pl.multiple_of(r0 * GRID_W, GRID_W)
            qs = pl.multiple_of(lr * GRID_W, GRID_W)
            for hp in range(NA_HEADS // 2):
                chains.append((qs, ks, r0 - i + (NA_KH - 1), hp,
                               slice(hp * LANES, (hp + 1) * LANES)))
        ss = [_pair_scores(q_ref[0, pl.ds(qs, GRID_W), cs], k_ref[0, pl.ds(ks, NA_BAND), cs])
              for qs, ks, dl, hp, cs in chains]
        parts = [_softmax_parts(s + tab_ref[hp, dl])
                 for s, (qs, ks, dl, hp, cs) in zip(ss, chains)]
        for (_, p), (qs, ks, dl, hp, cs) in zip(parts, chains):
            pv, den = _pv_and_den(p, v_ref[0, pl.ds(ks, NA_BAND), cs])
            o_ref[0, pl.ds(qs, GRID_W), cs] = _unstack_heads(pv / den).astype(BF16)
        return carry

    lax.fori_loop(0, NA_ROWS_PER_STEP // NA_GROUP, group_body, 0)


def _na_attention(q, k, v, rp, layer, b, s):
    rows = s // GRID_W
    qblk = NA_ROWS_PER_STEP * GRID_W
    q3, k3, v3 = (a.reshape(b, s, D_A) for a in (q, k, v))
    out = pl.pallas_call(
        functools.partial(_na_kernel, rows=rows),
        grid=(b, rows // NA_ROWS_PER_STEP),
        in_specs=[
            pl.BlockSpec((1, qblk, D_A), lambda i, j: (i, j, 0)),
            pl.BlockSpec((1, s, D_A), lambda i, j: (i, 0, 0)),
            pl.BlockSpec((1, s, D_A), lambda i, j: (i, 0, 0)),
            _layer_param(rp.shape[1:], layer),
        ],
        out_specs=pl.BlockSpec((1, qblk, D_A), lambda i, j: (i, j, 0)),
        out_shape=jax.ShapeDtypeStruct((b, s, D_A), BF16),
        scratch_shapes=[pltpu.VMEM((NA_HEADS, 2 * NA_KH - 1, GRID_W, LANES), F32),
                        pltpu.VMEM((NA_HEADS // 2, NA_KH, 2 * GRID_W, NA_BAND), F32)],
        compiler_params=_params(2),
        name="na_attn",
    )(q3, k3, v3, rp)
    return out.reshape(b * s, D_A)


DIL_QB = 128
DIL_HALF = 64
DIL_KB = DIL_QB + 2 * DIL_HALF
DIL_GROUP = 8
DIL_KEY_SHIFTS = (0, -1, -2)


def _alibi_slope(head):
    return 2.0 ** (-8.0 * (head + 1) / DIL_HEADS)


def _dil_kernel(x_ref, o_ref, lse_ref, bias_ref, *, length, dilation, slopes, residues):
    nblk = length // DIL_QB

    @pl.when((pl.program_id(0) == 0) & (pl.program_id(1) == 0))
    def _():
        rel0 = (lax.broadcasted_iota(jnp.int32, (DIL_QB, DIL_KB), 1)
                - lax.broadcasted_iota(jnp.int32, (DIL_QB, DIL_KB), 0))
        for c, shift in enumerate(DIL_KEY_SHIFTS):
            dist = jnp.abs(rel0 + shift * DIL_HALF)
            dist_f = (dist * dilation).astype(F32)
            for hh in range(2):
                bias = jnp.where(dist <= DIL_HALF, (-slopes[hh] * LOG2E) * dist_f, NEG_INF)
                bias_ref[c, hh * DIL_QB:(hh + 1) * DIL_QB, :] = bias

    def group(chains):
        geo = []
        for r, bi in chains:
            q0 = bi * DIL_QB
            if isinstance(bi, int):
                ks = min(max(q0 - DIL_HALF, 0), length - DIL_KB)
            else:
                q0 = pl.multiple_of(q0, DIL_QB)
                ks = pl.multiple_of(jnp.clip(q0 - DIL_HALF, 0, length - DIL_KB), DIL_HALF)
            geo.append((r, q0, ks, (q0 - ks) // DIL_HALF))
        ss = [_pair_scores(x_ref[0, r, pl.ds(q0, DIL_QB), 0:LANES],
                           x_ref[0, r, pl.ds(ks, DIL_KB), LANES:2 * LANES])
              for r, q0, ks, case in geo]
        parts = [_softmax_parts(s + bias_ref[case]) for s, (r, q0, ks, case) in zip(ss, geo)]
        for (m, p), (r, q0, ks, case) in zip(parts, geo):
            pv, den = _pv_and_den(p, x_ref[0, r, pl.ds(ks, DIL_KB), 2 * LANES:3 * LANES])
            o_ref[0, r, pl.ds(q0, DIL_QB), :] = _unstack_heads(pv / den).astype(BF16)
            lse_ref[0, r, pl.ds(q0, DIL_QB), :] = _unstack_heads(m * LN2 + jnp.log(den))

    if residues * nblk <= DIL_GROUP:
        group([(r, bi) for r in range(residues) for bi in range(nblk)])
    else:
        assert nblk % DIL_GROUP == 0
        for r in range(residues):
            def body(gi, carry, r=r):
                group([(r, gi * DIL_GROUP + j) for j in range(DIL_GROUP)])
                return carry
            lax.fori_loop(0, nblk // DIL_GROUP, body, 0)


def _dil_attention(qkv, grp):
    window, dilation = DIL_PAIRS[grp]
    assert window // 2 // dilation == DIL_HALF
    b, _, length, width = qkv.shape
    assert length % DIL_QB == 0 and length >= DIL_KB
    nblk = length // DIL_QB
    residues = min(dilation, max(1, DIL_GROUP // nblk))
    assert dilation % residues == 0
    slopes = tuple(_alibi_slope(2 * grp + hh) for hh in range(2))
    blk = lambda w: pl.BlockSpec((1, residues, length, w), lambda i, r: (i, r, 0, 0))
    return pl.pallas_call(
        functools.partial(_dil_kernel, length=length, dilation=dilation, slopes=slopes,
                          residues=residues),
        grid=(b, dilation // residues),
        in_specs=[blk(width)],
        out_specs=[blk(LANES), blk(LANES)],
        out_shape=[jax.ShapeDtypeStruct((b, dilation, length, LANES), BF16),
                   jax.ShapeDtypeStruct((b, dilation, length, LANES), F32)],
        scratch_shapes=[pltpu.VMEM((len(DIL_KEY_SHIFTS), 2 * DIL_QB, DIL_KB), F32)],
        compiler_params=_params(2),
        name=f"dil_attn_{grp}",
    )(qkv)


def _merge_kernel(x_ref, oa_ref, o0_ref, o1_ref, o2_ref, l0_ref, l1_ref, l2_ref, qm_ref, gate_ref,
                  kv_ref, wa_ref, wb_ref, wm_ref, wo_ref, g_ref, out_ref, stage_ref):
    pairs = range(MEM_HEADS // 2)
    ss = [_pair_scores(qm_ref[:, hp * LANES:(hp + 1) * LANES],
                       kv_ref[0, 0, :, hp * LANES:(hp + 1) * LANES]) for hp in pairs]
    proj_a = _dot(oa_ref[...], wa_ref[0])

    def natural_order(ref, grp, slot):
        d = DIL_PAIRS[grp][1]
        if d == 1:
            return ref[0, 0].astype(F32)
        stage = stage_ref.at[slot]
        for r in range(d):
            stage[pl.ds(r, ROW_TILE // d, stride=d), :] = ref[0, r].astype(F32)
        return stage[...]

    o_refs, l_refs = (o0_ref, o1_ref, o2_ref), (l0_ref, l1_ref, l2_ref)
    os_ = [natural_order(o_refs[g], g, 2 * (g - 1)) for g in range(_N_GROUPS)]
    ls = [natural_order(l_refs[g], g, 2 * (g - 1) + 1) for g in range(_N_GROUPS)]
    mx = jnp.maximum(jnp.maximum(ls[0], ls[1]), ls[2])
    es = [jnp.exp(l - mx) for l in ls]
    inv = 1.0 / (es[0] + es[1] + es[2])
    o_b = ((es[0] * inv) * os_[0] + (es[1] * inv) * os_[1] + (es[2] * inv) * os_[2]).astype(BF16)
    proj_b = _dot(o_b, wb_ref[0])

    om = []
    for hp, s in zip(pairs, ss):
        _, p = _softmax_parts(s)
        pv, den = _pv_and_den(p, kv_ref[0, 0, :, D_M + hp * LANES:D_M + (hp + 1) * LANES])
        om.append(_unstack_heads(pv / den).astype(BF16))
    o_m = jnp.concatenate(om, axis=-1)

    merged = (gate_ref[:, 0:D_MODEL].astype(F32) * proj_a
              + gate_ref[:, D_MODEL:2 * D_MODEL].astype(F32) * proj_b
              + gate_ref[:, 2 * D_MODEL:3 * D_MODEL].astype(F32) * _dot(o_m, wm_ref[0]))
    merged = merged.astype(BF16)
    for r0 in range(0, ROW_TILE, FF_OUT_ROWS):
        rows = slice(r0, r0 + FF_OUT_ROWS)
        y = _dot(merged[rows], wo_ref[0])
        out_ref[rows, :] = x_ref[rows, :] + _rms(y, g_ref[0])


def _merge(x2, oa, ob, lse, qm, gates, kv, wa, wb, wm, wo, g, layer, s):
    t = x2.shape[0]
    tiles_per_seq = s // ROW_TILE
    row = lambda i: (i, 0)
    dil = [pl.BlockSpec((1, d, ROW_TILE // d, LANES),
                        lambda i: (i // tiles_per_seq, 0, i % tiles_per_seq, 0))
           for _, d in DIL_PAIRS]
    in_specs = [
        pl.BlockSpec((ROW_TILE, D_MODEL), row),
        pl.BlockSpec((ROW_TILE, D_A), row),
        *dil, *dil,
        pl.BlockSpec((ROW_TILE, D_M), row),
        pl.BlockSpec((ROW_TILE, N_BRANCH * D_MODEL), row),
        pl.BlockSpec((1, 1, N_MEM, 2 * D_M), lambda i: (layer, i // tiles_per_seq, 0, 0)),
        _layer_param((D_A, D_MODEL), layer), _layer_param((D_B_OUT, D_MODEL), layer),
        _layer_param((D_M, D_MODEL), layer), _layer_param((D_MODEL, D_MODEL), layer),
        _layer_param((1, D_MODEL), layer),
    ]
    return pl.pallas_call(
        _merge_kernel,
        grid=(t // ROW_TILE,),
        in_specs=in_specs,
        out_specs=pl.BlockSpec((ROW_TILE, D_MODEL), row),
        out_shape=jax.ShapeDtypeStruct((t, D_MODEL), F32),
        scratch_shapes=[pltpu.VMEM((2 * (_N_GROUPS - 1), ROW_TILE, LANES), F32)],
        compiler_params=_params(),
        name="merge",
    )(x2, oa, *ob, *lse, qm, gates, kv, wa, wb, wm, wo, g)


FF_CHUNK = 256
FF_OUT_ROWS = 256
GELU_C0 = 0.7978845608028654
GELU_C1 = GELU_C0 * 0.044715


def _gelu_gate(a, half_b):
    t = jnp.tanh(a * (GELU_C0 + GELU_C1 * (a * a)))
    hb = a * half_b
    return hb + hb * t


def _ffn_kernel(xm_ref, xp_ref, xn_ref, g1_ref, wup_ref, cw_ref, cb_ref, wdn_ref, g2_ref, out_ref,
                hext_ref, f_ref, *, tiles_per_seq):
    i = pl.program_id(0)
    pos = i % tiles_per_seq
    g1 = g1_ref[0]
    x = xm_ref[...]
    hext_ref[0:FF_TM, :] = _rms(x, g1).astype(BF16)
    hn = jnp.where(pos == tiles_per_seq - 1, 0.0, _rms(xn_ref[...], g1))
    hp = jnp.where(pos == 0, 0.0, _rms(xp_ref[...], g1))
    hext_ref[FF_TM:FF_TM + 2 * SUB, :] = jnp.concatenate([hn, hp], axis=0).astype(BF16)
    hext = hext_ref[...]
    ext = FF_TM + 2 * SUB

    def conv(u, col):
        cs = slice(col, col + FF_CHUNK)
        before = pltpu.roll(u, 1, 0)[0:FF_TM]
        after = pltpu.roll(u, ext - 1, 0)[0:FF_TM]
        y = cb_ref[0, :, cs] + before * cw_ref[0, 0:1, cs]
        y = y + u[0:FF_TM] * cw_ref[0, 1:2, cs]
        return y + after * cw_ref[0, 2:3, cs]

    for c in range(D_FF // FF_CHUNK):
        ca, cb = c * FF_CHUNK, D_FF + c * FF_CHUNK
        ua = conv(_dot(hext, wup_ref[0, :, ca:ca + FF_CHUNK]), ca)
        ub = conv(_dot(hext, wup_ref[0, :, cb:cb + FF_CHUNK]), cb)
        f_ref[:, ca:ca + FF_CHUNK] = _gelu_gate(ua, ub).astype(BF16)

    y = _dot(f_ref[...], wdn_ref[0])
    out_ref[...] = x + _rms(y, g2_ref[0])


def _ffn(x2, g1, w_up, cw, cb, w_dn, g2, layer, s):
    t = x2.shape[0]
    assert s % FF_TM == 0
    tiles_per_seq = s // FF_TM
    per = FF_TM // SUB
    nh = t // SUB
    row = lambda i: (i, 0)
    return pl.pallas_call(
        functools.partial(_ffn_kernel, tiles_per_seq=tiles_per_seq),
        grid=(t // FF_TM,),
        in_specs=[
            pl.BlockSpec((FF_TM, D_MODEL), row),
            pl.BlockSpec((SUB, D_MODEL), lambda i: (jnp.maximum(i * per - 1, 0), 0)),
            pl.BlockSpec((SUB, D_MODEL), lambda i: (jnp.minimum((i + 1) * per, nh - 1), 0)),
            _layer_param((1, D_MODEL), layer),
            _layer_param((D_MODEL, 2 * D_FF), layer),
            _layer_param((3, 2 * D_FF), layer),
            _layer_param((1, 2 * D_FF), layer),
            _layer_param((D_FF, D_MODEL), layer),
            _layer_param((1, D_MODEL), layer),
        ],
        out_specs=pl.BlockSpec((FF_TM, D_MODEL), row),
        out_shape=jax.ShapeDtypeStruct((t, D_MODEL), F32),
        scratch_shapes=[pltpu.VMEM((FF_TM + 2 * SUB, D_MODEL), BF16),
                        pltpu.VMEM((FF_TM, D_FF), BF16)],
        compiler_params=_params(),
        name="ffn",
    )(x2, x2, x2, g1, w_up, cw, cb, w_dn, g2)


def kernel(x, mem, mem_norm_g, g_pre_mix, w_in, rpb_na, w_mem_kv, b_gate, w_br_a, w_br_b, w_br_m,
           w_out, g_post_mix, g_pre_ffn, w_up, conv_w, conv_b, w_down, g_post_ffn):
    b, s, d = x.shape
    depth = w_in.shape[0]
    assert d == D_MODEL and s % GRID_W == 0 and s % ROW_TILE == 0
    rows = s // GRID_W
    assert rows >= NA_KH and rows % NA_ROWS_PER_STEP == 0

    bf = lambda a: a.astype(BF16)
    vec = lambda a: a.reshape(depth, 1, -1).astype(F32)

    w_in, w_up, w_down, w_out = bf(w_in), bf(w_up), bf(w_down), bf(w_out)
    w_br_a, w_br_b, w_br_m = bf(w_br_a), bf(w_br_b), bf(w_br_m)
    g_pre_mix, g_post_mix, g_pre_ffn, g_post_ffn = map(vec, (g_pre_mix, g_post_mix, g_pre_ffn,
                                                             g_post_ffn))
    half = jnp.concatenate([jnp.ones((D_FF,), F32), jnp.full((D_FF,), 0.5, F32)])
    b_gate, conv_b, conv_w = vec(b_gate), vec(conv_b * half), conv_w.astype(F32) * half
    assert rpb_na.shape[1:] == (NA_HEADS, 2 * NA_KH - 1, 2 * NA_KW - 1)
    rp = jnp.pad(rpb_na.astype(F32) * LOG2E, ((0, 0), (0, 0), (0, 0),
                                      (NA_RPB_PAD, LANES - NA_RPB_PAD - (2 * NA_KW - 1))))
    kv = _memkv(mem, mem_norm_g.reshape(1, -1).astype(F32), bf(w_mem_kv))

    x2 = x.reshape(b * s, d)
    for l in range(depth):
        qa, ka, va, b0, b1, b2, qm, gates = _inproj(x2, g_pre_mix, w_in, b_gate, l, b, s)
        oa = _na_attention(qa, ka, va, rp, l, b, s)
        ob, lse = zip(*(_dil_attention(q, grp) for grp, q in enumerate((b0, b1, b2))))
        x2 = _merge(x2, oa, ob, lse, qm, gates, kv, w_br_a, w_br_b, w_br_m, w_out, g_post_mix, l, s)
        x2 = _ffn(x2, g_pre_ffn, w_up, conv_w, conv_b, w_down, g_post_ffn, l, s)
    return x2.reshape(b, s, d)
```

```python
import functools

import jax
import jax.numpy as jnp
from jax import lax
from jax.experimental import pallas as pl
from jax.experimental.pallas import tpu as pltpu

D_MODEL = 1024
GRID_W = 64
N_MEM = 256
HEAD_DIM = 64
NA_HEADS = 6
NA_KH = 8
NA_KW = 16
DIL_PAIRS = ((128, 1), (512, 4), (2048, 16))
DIL_HEADS = 6
MEM_HEADS = 4
D_A = 384
D_B = 384
D_B_OUT = 128
D_M = 256
N_BRANCH = 3
D_IN = 3 * D_A + 3 * D_B + D_M + N_BRANCH * D_MODEL
D_FF = 2816
RMS_EPS = 1e-6
NEG_INF = -1e30

LANES = 128
SUB = 8
SUB_BF16 = 16
VMEM_LIMIT = 56 * 1024 * 1024
IN_TM = 1024
IN_TM_FIRST = 512
MG_TM = 1024
FF_TM = 1024

F32 = jnp.float32
BF16 = jnp.bfloat16

LOG2E = 1.4426950408889634
LN2 = 0.6931471805599453
Q_SCALE = HEAD_DIM ** -0.5 * LOG2E


def _rms(x, g):
    return x * lax.rsqrt(jnp.mean(x * x, axis=-1, keepdims=True) + RMS_EPS) * g


def _dot(a, b):
    return jnp.dot(a, b, preferred_element_type=F32)


def _dot_nt(a, b):
    return lax.dot_general(a, b, (((1,), (1,)), ((), ())), preferred_element_type=F32)


def _layer_param(shape, layer):
    zeros = (0,) * len(shape)
    return pl.BlockSpec((1,) + tuple(shape), lambda *_: (layer,) + zeros,
                        pipeline_mode=pl.Buffered(1))


def _params(n_axes=1):
    return pltpu.CompilerParams(dimension_semantics=("arbitrary",) * n_axes,
                                vmem_limit_bytes=VMEM_LIMIT)


def _memkv_kernel(mem_ref, g_ref, w_ref, o_ref):
    mn = _rms(mem_ref[0], g_ref[...]).astype(BF16)
    o_ref[0, 0] = _dot(mn, w_ref[0]).astype(BF16)


def _memkv(mem, g, w_kv):
    depth = w_kv.shape[0]
    b = mem.shape[0]
    return pl.pallas_call(
        _memkv_kernel,
        grid=(depth, b),
        in_specs=[
            pl.BlockSpec((1, N_MEM, D_MODEL), lambda l, i: (i, 0, 0)),
            pl.BlockSpec((1, D_MODEL), lambda l, i: (0, 0)),
            pl.BlockSpec((1, D_MODEL, 2 * D_M), lambda l, i: (l, 0, 0)),
        ],
        out_specs=pl.BlockSpec((1, 1, N_MEM, 2 * D_M), lambda l, i: (l, i, 0, 0)),
        out_shape=jax.ShapeDtypeStruct((depth, b, N_MEM, 2 * D_M), BF16),
        compiler_params=_params(2),
        name="memkv",
    )(mem, g, w_kv)


IN_CHUNK = 512
_A_SLABS = D_A // LANES
_M_SLABS = D_M // LANES
_N_GROUPS = len(DIL_PAIRS)


def _inproj_kernel(*refs, n_casts):
    x_ref, g_ref, w_ref, bg_ref = refs[:4]
    cast_in = refs[4:4 + n_casts]
    qa_ref, ka_ref, va_ref, b0_ref, b1_ref, b2_ref, qm_ref, gate_ref = refs[4 + n_casts:12 + n_casts]
    cast_out = refs[12 + n_casts:12 + 2 * n_casts]
    stage_ref = refs[-1]
    tm = x_ref.shape[0]
    x = x_ref[...]
    h = (x * g_ref[0]).astype(BF16)
    inv = jnp.broadcast_to(lax.rsqrt(jnp.mean(x * x, axis=-1, keepdims=True) + RMS_EPS),
                           (tm, LANES))
    inv_q = inv * Q_SCALE
    b_refs = (b0_ref, b1_ref, b2_ref)

    def put_dilated(which, grp, val):
        d = DIL_PAIRS[grp][1]
        cs = slice(which * LANES, (which + 1) * LANES)
        if d == 1:
            b_refs[grp][0, 0, :, cs] = val.astype(BF16)
            return
        stage = stage_ref.at[(grp - 1) * 3 + which]
        stage[...] = val
        for r in range(d):
            b_refs[grp][0, r, :, cs] = stage[pl.ds(r, tm // d, stride=d), :].astype(BF16)

    def put(slab, val):
        s = slab
        if s < 3 * _A_SLABS:
            which, j = divmod(s, _A_SLABS)
            ref = (qa_ref, ka_ref, va_ref)[which]
            ref[:, j * LANES:(j + 1) * LANES] = (val * (inv_q if which == 0 else inv)).astype(BF16)
            return
        s -= 3 * _A_SLABS
        if s < 3 * _N_GROUPS:
            which, grp = divmod(s, _N_GROUPS)
            put_dilated(which, grp, val * (inv_q if which == 0 else inv))
            return
        s -= 3 * _N_GROUPS
        if s < _M_SLABS:
            qm_ref[:, s * LANES:(s + 1) * LANES] = (val * inv_q).astype(BF16)
            return
        s -= _M_SLABS
        gate = jax.nn.sigmoid(val * inv + bg_ref[0, :, s * LANES:(s + 1) * LANES])
        gate_ref[:, s * LANES:(s + 1) * LANES] = gate.astype(BF16)

    per = IN_CHUNK // LANES
    for c in range(D_IN // IN_CHUNK):
        r = _dot(h, w_ref[0, :, c * IN_CHUNK:(c + 1) * IN_CHUNK])
        for j in range(per):
            put(c * per + j, r[:, j * LANES:(j + 1) * LANES])

    for src, dst in zip(cast_in, cast_out):
        dst[...] = src[...].astype(BF16)


def _inproj(x2, g, w_in, b_gate, layer, g_layer, b, s, tm, casts=()):
    t = x2.shape[0]
    assert s % tm == 0
    tiles_per_seq = s // tm
    steps = t // tm
    row = lambda i: (i, 0)
    outs = [jax.ShapeDtypeStruct((t, D_A), BF16)] * 3
    out_specs = [pl.BlockSpec((tm, D_A), row)] * 3
    for _, d in DIL_PAIRS:
        assert tm % (d * SUB_BF16) == 0
        outs.append(jax.ShapeDtypeStruct((b, d, s // d, 3 * LANES), BF16))
        out_specs.append(pl.BlockSpec((1, d, tm // d, 3 * LANES),
                                      lambda i: (i // tiles_per_seq, 0, i % tiles_per_seq, 0)))
    outs += [jax.ShapeDtypeStruct((t, D_M), BF16),
             jax.ShapeDtypeStruct((t, N_BRANCH * D_MODEL), BF16)]
    out_specs += [pl.BlockSpec((tm, D_M), row),
                  pl.BlockSpec((tm, N_BRANCH * D_MODEL), row)]
    cast_in_specs, cast_out_specs = [], []
    for a, skip in casts:
        rows, cols = a.shape
        blk = (rows - skip) // steps
        assert (rows - skip) % (steps * SUB_BF16) == 0 and skip % blk == 0
        cast_in_specs.append(pl.BlockSpec((blk, cols), lambda i, o=skip // blk: (o + i, 0)))
        cast_out_specs.append(pl.BlockSpec((blk, cols), row))
        outs.append(jax.ShapeDtypeStruct((rows - skip, cols), BF16))
    return pl.pallas_call(
        functools.partial(_inproj_kernel, n_casts=len(casts)),
        grid=(steps,),
        in_specs=[
            pl.BlockSpec((tm, D_MODEL), row),
            _layer_param((1, D_MODEL), g_layer),
            _layer_param((D_MODEL, D_IN), layer),
            _layer_param((1, N_BRANCH * D_MODEL), g_layer),
            *cast_in_specs,
        ],
        out_specs=out_specs + cast_out_specs,
        out_shape=outs,
        scratch_shapes=[pltpu.VMEM((3 * (_N_GROUPS - 1), tm, LANES), F32)],
        compiler_params=_params(),
        name="inproj",
    )(x2, g, w_in, b_gate, *(a for a, _ in casts))


def _pair_scores(q2, k2):
    lane = lax.broadcasted_iota(jnp.int32, q2.shape, 1)
    zero = jnp.zeros_like(q2)
    qs = jnp.concatenate([jnp.where(lane < HEAD_DIM, q2, zero),
                          jnp.where(lane < HEAD_DIM, zero, q2)], axis=0)
    return _dot_nt(qs, k2)


def _softmax_parts(s):
    m = jnp.max(s, axis=-1, keepdims=True)
    return m, jnp.exp2(s - m).astype(BF16)


def _pv_and_den(p, v2):
    r = _dot(p, jnp.concatenate([v2, jnp.ones_like(v2)], axis=1))
    return r[:, :LANES], r[:, LANES:]


def _unstack_heads(a):
    m_rows = a.shape[0] // 2
    lane = lax.broadcasted_iota(jnp.int32, (m_rows, LANES), 1)
    return jnp.where(lane < HEAD_DIM, a[:m_rows], a[m_rows:])


NA_ROWS_PER_STEP = 8
NA_GROUP = 4
NA_BAND = NA_KH * GRID_W


NA_RPB_PAD = GRID_W - NA_KW


def _na_build_bias(rp_ref, toe_ref, tab_ref):
    lane = lax.broadcasted_iota(jnp.int32, (GRID_W, LANES), 1)
    q = lax.broadcasted_iota(jnp.int32, (GRID_W, LANES), 0)
    c = lane % GRID_W
    c0 = jnp.clip(q - NA_KW // 2, 0, GRID_W - NA_KW)
    valid = (c >= c0) & (c < c0 + NA_KW)
    base_lo, base_hi = LANES - (GRID_W - 1), LANES - (GRID_W - 1) - GRID_W

    def toe_body(ro, carry):
        for h in range(NA_HEADS):
            row = jnp.broadcast_to(rp_ref[0, h, pl.ds(ro, 1), :], (GRID_W, LANES))
            lo = pltpu.roll(row, base_lo, 1, stride=1, stride_axis=0)
            hi = pltpu.roll(row, base_hi, 1, stride=1, stride_axis=0)
            toe_ref[h, ro] = jnp.where(valid, jnp.where(lane < GRID_W, lo, hi), NEG_INF)
        return carry

    lax.fori_loop(0, 2 * NA_KH - 1, toe_body, 0)

    def tab_body(dl, carry):
        for h in range(NA_HEADS):
            rows_ = slice((h % 2) * GRID_W, (h % 2 + 1) * GRID_W)
            for g in range(NA_KH // 2):
                even, odd = toe_ref[h, dl + 2 * g], toe_ref[h, dl + 2 * g + 1]
                tab_ref[h // 2, dl, rows_, g * LANES:(g + 1) * LANES] = \
                    jnp.where(lane < GRID_W, even, odd)
        return carry

    lax.fori_loop(0, NA_KH, tab_body, 0)


def _na_kernel(q_ref, k_ref, v_ref, rp_ref, o_ref, toe_ref, tab_ref, *, rows):
    rb = pl.program_id(1)

    @pl.when((pl.program_id(0) == 0) & (rb == 0))
    def _():
        _na_build_bias(rp_ref, toe_ref, tab_ref)

    def group_body(gi, carry):
        chains = []
        for lg in range(NA_GROUP):
            lr = gi * NA_GROUP + lg
            i = rb * NA_ROWS_PER_STEP + lr
            r0 = jnp.clip(i - NA_KH // 2, 0, rows - NA_KH)
            ks = pl.multiple_of(r0 * GRID_W, GRID_W)
            qs = pl.multiple_of(lr * GRID_W, GRID_W)
            for hp in range(NA_HEADS // 2):
                chains.append((qs, ks, r0 - i + (NA_KH - 1), hp,
                               slice(hp * LANES, (hp + 1) * LANES)))
        ss = [_pair_scores(q_ref[0, pl.ds(qs, GRID_W), cs], k_ref[0, pl.ds(ks, NA_BAND), cs])
              for qs, ks, dl, hp, cs in chains]
        parts = [_softmax_parts(s + tab_ref[hp, dl])
                 for s, (qs, ks, dl, hp, cs) in zip(ss, chains)]
        for (_, p), (qs, ks, dl, hp, cs) in zip(parts, chains):
            pv, den = _pv_and_den(p, v_ref[0, pl.ds(ks, NA_BAND), cs])
            o_ref[0, pl.ds(qs, GRID_W), cs] = _unstack_heads(pv / den).astype(BF16)
        return carry

    lax.fori_loop(0, NA_ROWS_PER_STEP // NA_GROUP, group_body, 0)


def _na_attention(q, k, v, rp, layer, b, s):
    rows = s // GRID_W
    qblk = NA_ROWS_PER_STEP * GRID_W
    q3, k3, v3 = (a.reshape(b, s, D_A) for a in (q, k, v))
    out = pl.pallas_call(
        functools.partial(_na_kernel, rows=rows),
        grid=(b, rows // NA_ROWS_PER_STEP),
        in_specs=[
            pl.BlockSpec((1, qblk, D_A), lambda i, j: (i, j, 0)),
            pl.BlockSpec((1, s, D_A), lambda i, j: (i, 0, 0)),
            pl.BlockSpec((1, s, D_A), lambda i, j: (i, 0, 0)),
            _layer_param(rp.shape[1:], layer),
        ],
        out_specs=pl.BlockSpec((1, qblk, D_A), lambda i, j: (i, j, 0)),
        out_shape=jax.ShapeDtypeStruct((b, s, D_A), BF16),
        scratch_shapes=[pltpu.VMEM((NA_HEADS, 2 * NA_KH - 1, GRID_W, LANES), F32),
                        pltpu.VMEM((NA_HEADS // 2, NA_KH, 2 * GRID_W, NA_BAND), F32)],
        compiler_params=_params(2),
        name="na_attn",
    )(q3, k3, v3, rp)
    return out.reshape(b * s, D_A)


DIL_QB = 128
DIL_HALF = 64
DIL_KB = DIL_QB + 2 * DIL_HALF
DIL_GROUP = 8
DIL_KEY_SHIFTS = (0, -1, -2)


def _alibi_slope(head):
    return 2.0 ** (-8.0 * (head + 1) / DIL_HEADS)


def _dil_kernel(x_ref, o_ref, lse_ref, bias_ref, *, length, dilation, slopes, residues):
    nblk = length // DIL_QB

    @pl.when((pl.program_id(0) == 0) & (pl.program_id(1) == 0))
    def _():
        rel0 = (lax.broadcasted_iota(jnp.int32, (DIL_QB, DIL_KB), 1)
                - lax.broadcasted_iota(jnp.int32, (DIL_QB, DIL_KB), 0))
        for c, shift in enumerate(DIL_KEY_SHIFTS):
            dist = jnp.abs(rel0 + shift * DIL_HALF)
            dist_f = (dist * dilation).astype(F32)
            for hh in range(2):
                bias = jnp.where(dist <= DIL_HALF, (-slopes[hh] * LOG2E) * dist_f, NEG_INF)
                bias_ref[c, hh * DIL_QB:(hh + 1) * DIL_QB, :] = bias

    def group(chains):
        geo = []
        for r, bi in chains:
            q0 = bi * DIL_QB
            if isinstance(bi, int):
                ks = min(max(q0 - DIL_HALF, 0), length - DIL_KB)
            else:
                q0 = pl.multiple_of(q0, DIL_QB)
                ks = pl.multiple_of(jnp.clip(q0 - DIL_HALF, 0, length - DIL_KB), DIL_HALF)
            geo.append((r, q0, ks, (q0 - ks) // DIL_HALF))
        ss = [_pair_scores(x_ref[0, r, pl.ds(q0, DIL_QB), 0:LANES],
                           x_ref[0, r, pl.ds(ks, DIL_KB), LANES:2 * LANES])
              for r, q0, ks, case in geo]
        parts = [_softmax_parts(s + bias_ref[case]) for s, (r, q0, ks, case) in zip(ss, geo)]
        for (m, p), (r, q0, ks, case) in zip(parts, geo):
            pv, den = _pv_and_den(p, x_ref[0, r, pl.ds(ks, DIL_KB), 2 * LANES:3 * LANES])
            o_ref[0, r, pl.ds(q0, DIL_QB), :] = _unstack_heads(pv / den).astype(BF16)
            lse_ref[0, r, pl.ds(q0, DIL_QB), :] = _unstack_heads(m * LN2 + jnp.log(den))

    if residues * nblk <= DIL_GROUP:
        group([(r, bi) for r in range(residues) for bi in range(nblk)])
    else:
        assert nblk % DIL_GROUP == 0
        for r in range(residues):
            def body(gi, carry, r=r):
                group([(r, gi * DIL_GROUP + j) for j in range(DIL_GROUP)])
                return carry
            lax.fori_loop(0, nblk // DIL_GROUP, body, 0)


def _dil_attention(qkv, grp):
    window, dilation = DIL_PAIRS[grp]
    assert window // 2 // dilation == DIL_HALF
    b, _, length, width = qkv.shape
    assert length % DIL_QB == 0 and length >= DIL_KB
    nblk = length // DIL_QB
    residues = min(dilation, max(1, DIL_GROUP // nblk))
    assert dilation % residues == 0
    slopes = tuple(_alibi_slope(2 * grp + hh) for hh in range(2))
    blk = lambda w: pl.BlockSpec((1, residues, length, w), lambda i, r: (i, r, 0, 0))
    return pl.pallas_call(
        functools.partial(_dil_kernel, length=length, dilation=dilation, slopes=slopes,
                          residues=residues),
        grid=(b, dilation // residues),
        in_specs=[blk(width)],
        out_specs=[blk(LANES), blk(LANES)],
        out_shape=[jax.ShapeDtypeStruct((b, dilation, length, LANES), BF16),
                   jax.ShapeDtypeStruct((b, dilation, length, LANES), F32)],
        scratch_shapes=[pltpu.VMEM((len(DIL_KEY_SHIFTS), 2 * DIL_QB, DIL_KB), F32)],
        compiler_params=_params(2),
        name=f"dil_attn_{grp}",
    )(qkv)


def _merge_kernel(x_ref, oa_ref, o0_ref, o1_ref, o2_ref, l0_ref, l1_ref, l2_ref, qm_ref, gate_ref,
                  kv_ref, wa_ref, wb_ref, wm_ref, wo_ref, g_ref, out_ref, stage_ref):
    pairs = range(MEM_HEADS // 2)
    ss = [_pair_scores(qm_ref[:, hp * LANES:(hp + 1) * LANES],
                       kv_ref[0, 0, :, hp * LANES:(hp + 1) * LANES]) for hp in pairs]
    proj_a = _dot(oa_ref[...], wa_ref[0])

    def natural_order(ref, grp, slot):
        d = DIL_PAIRS[grp][1]
        if d == 1:
            return ref[0, 0].astype(F32)
        stage = stage_ref.at[slot]
        for r in range(d):
            stage[pl.ds(r, MG_TM // d, stride=d), :] = ref[0, r].astype(F32)
        return stage[...]

    o_refs, l_refs = (o0_ref, o1_ref, o2_ref), (l0_ref, l1_ref, l2_ref)
    os_ = [natural_order(o_refs[g], g, 2 * (g - 1)) for g in range(_N_GROUPS)]
    ls = [natural_order(l_refs[g], g, 2 * (g - 1) + 1) for g in range(_N_GROUPS)]
    mx = jnp.maximum(jnp.maximum(ls[0], ls[1]), ls[2])
    es = [jnp.exp(l - mx) for l in ls]
    inv = 1.0 / (es[0] + es[1] + es[2])
    o_b = ((es[0] * inv) * os_[0] + (es[1] * inv) * os_[1] + (es[2] * inv) * os_[2]).astype(BF16)
    proj_b = _dot(o_b, wb_ref[0])

    om = []
    for hp, s in zip(pairs, ss):
        _, p = _softmax_parts(s)
        pv, den = _pv_and_den(p, kv_ref[0, 0, :, D_M + hp * LANES:D_M + (hp + 1) * LANES])
        om.append(_unstack_heads(pv / den).astype(BF16))
    o_m = jnp.concatenate(om, axis=-1)

    merged = (gate_ref[:, 0:D_MODEL].astype(F32) * proj_a
              + gate_ref[:, D_MODEL:2 * D_MODEL].astype(F32) * proj_b
              + gate_ref[:, 2 * D_MODEL:3 * D_MODEL].astype(F32) * _dot(o_m, wm_ref[0]))
    merged = merged.astype(BF16)
    for r0 in range(0, MG_TM, MG_OUT_ROWS):
        rows = slice(r0, r0 + MG_OUT_ROWS)
        y = _dot(merged[rows], wo_ref[0])
        out_ref[rows, :] = x_ref[rows, :] + _rms(y, g_ref[0])


def _merge(x2, oa, ob, lse, qm, gates, kv, wa, wb, wm, wo, g, layer, s):
    t = x2.shape[0]
    tiles_per_seq = s // MG_TM
    row = lambda i: (i, 0)
    dil = [pl.BlockSpec((1, d, MG_TM // d, LANES),
                        lambda i: (i // tiles_per_seq, 0, i % tiles_per_seq, 0))
           for _, d in DIL_PAIRS]
    in_specs = [
        pl.BlockSpec((MG_TM, D_MODEL), row),
        pl.BlockSpec((MG_TM, D_A), row),
        *dil, *dil,
        pl.BlockSpec((MG_TM, D_M), row),
        pl.BlockSpec((MG_TM, N_BRANCH * D_MODEL), row),
        pl.BlockSpec((1, 1, N_MEM, 2 * D_M), lambda i: (layer, i // tiles_per_seq, 0, 0)),
        _layer_param((D_A, D_MODEL), layer), _layer_param((D_B_OUT, D_MODEL), layer),
        _layer_param((D_M, D_MODEL), layer), _layer_param((D_MODEL, D_MODEL), layer),
        _layer_param((1, D_MODEL), layer),
    ]
    return pl.pallas_call(
        _merge_kernel,
        grid=(t // MG_TM,),
        in_specs=in_specs,
        out_specs=pl.BlockSpec((MG_TM, D_MODEL), row),
        out_shape=jax.ShapeDtypeStruct((t, D_MODEL), F32),
        scratch_shapes=[pltpu.VMEM((2 * (_N_GROUPS - 1), MG_TM, LANES), F32)],
        compiler_params=_params(),
        name="merge",
    )(x2, oa, *ob, *lse, qm, gates, kv, wa, wb, wm, wo, g)


FF_CHUNK = 256
MG_OUT_ROWS = 256
GELU_C0 = 0.7978845608028654
GELU_C1 = GELU_C0 * 0.044715


def _gelu_gate(a, half_b):
    t = jnp.tanh(a * (GELU_C0 + GELU_C1 * (a * a)))
    hb = a * half_b
    return hb + hb * t


def _ffn_kernel(xm_ref, xp_ref, xn_ref, g1_ref, wup_ref, cw_ref, cb_ref, wdn_ref, g2_ref, out_ref,
                hext_ref, f_ref, *, tiles_per_seq):
    i = pl.program_id(0)
    pos = i % tiles_per_seq
    g1 = g1_ref[0]
    x = xm_ref[...]
    hext_ref[0:FF_TM, :] = _rms(x, g1).astype(BF16)
    hn = jnp.where(pos == tiles_per_seq - 1, 0.0, _rms(xn_ref[...], g1))
    hp = jnp.where(pos == 0, 0.0, _rms(xp_ref[...], g1))
    hext_ref[FF_TM:FF_TM + 2 * SUB, :] = jnp.concatenate([hn, hp], axis=0).astype(BF16)
    hext = hext_ref[...]
    ext = FF_TM + 2 * SUB

    def conv(u, col):
        cs = slice(col, col + FF_CHUNK)
        before = pltpu.roll(u, 1, 0)[0:FF_TM]
        after = pltpu.roll(u, ext - 1, 0)[0:FF_TM]
        y = cb_ref[0, :, cs] + before * cw_ref[0, 0:1, cs]
        y = y + u[0:FF_TM] * cw_ref[0, 1:2, cs]
        return y + after * cw_ref[0, 2:3, cs]

    for c in range(D_FF // FF_CHUNK):
        ca, cb = c * FF_CHUNK, D_FF + c * FF_CHUNK
        ua = conv(_dot(hext, wup_ref[0, :, ca:ca + FF_CHUNK]), ca)
        ub = conv(_dot(hext, wup_ref[0, :, cb:cb + FF_CHUNK]), cb)
        f_ref[:, ca:ca + FF_CHUNK] = _gelu_gate(ua, ub).astype(BF16)

    y = _dot(f_ref[...], wdn_ref[0])
    out_ref[...] = x + _rms(y, g2_ref[0])


def _ffn(x2, g1, w_up, cw, cb, w_dn, g2, layer, s):
    t = x2.shape[0]
    assert s % FF_TM == 0
    tiles_per_seq = s // FF_TM
    per = FF_TM // SUB
    nh = t // SUB
    row = lambda i: (i, 0)
    return pl.pallas_call(
        functools.partial(_ffn_kernel, tiles_per_seq=tiles_per_seq),
        grid=(t // FF_TM,),
        in_specs=[
            pl.BlockSpec((FF_TM, D_MODEL), row),
            pl.BlockSpec((SUB, D_MODEL), lambda i: (jnp.maximum(i * per - 1, 0), 0)),
            pl.BlockSpec((SUB, D_MODEL), lambda i: (jnp.minimum((i + 1) * per, nh - 1), 0)),
            _layer_param((1, D_MODEL), layer),
            _layer_param((D_MODEL, 2 * D_FF), layer),
            _layer_param((3, 2 * D_FF), layer),
            _layer_param((1, 2 * D_FF), layer),
            _layer_param((D_FF, D_MODEL), layer),
            _layer_param((1, D_MODEL), layer),
        ],
        out_specs=pl.BlockSpec((FF_TM, D_MODEL), row),
        out_shape=jax.ShapeDtypeStruct((t, D_MODEL), F32),
        scratch_shapes=[pltpu.VMEM((FF_TM + 2 * SUB, D_MODEL), BF16),
                        pltpu.VMEM((FF_TM, D_FF), BF16)],
        compiler_params=_params(),
        name="ffn",
    )(x2, x2, x2, g1, w_up, cw, cb, w_dn, g2)


def kernel(x, mem, mem_norm_g, g_pre_mix, w_in, rpb_na, w_mem_kv, b_gate, w_br_a, w_br_b, w_br_m,
           w_out, g_post_mix, g_pre_ffn, w_up, conv_w, conv_b, w_down, g_post_ffn):
    b, s, d = x.shape
    depth = w_in.shape[0]
    assert d == D_MODEL and s % GRID_W == 0 and s % MG_TM == 0
    rows = s // GRID_W
    assert rows >= NA_KH and rows % NA_ROWS_PER_STEP == 0

    bf = lambda a: a.astype(BF16)
    vec = lambda a: a.reshape(depth, 1, -1).astype(F32)

    w_in_first = bf(w_in[:1])
    late_casts = [(w_up.reshape(depth * D_MODEL, 2 * D_FF), 0),
                  (w_down.reshape(depth * D_FF, D_MODEL), 0),
                  (w_out.reshape(depth * D_MODEL, D_MODEL), 0)]
    if depth > 1:
        late_casts.append((w_in.reshape(depth * D_MODEL, D_IN), D_MODEL))
    w_br_a, w_br_b, w_br_m = bf(w_br_a), bf(w_br_b), bf(w_br_m)
    g_pre_mix, g_post_mix, g_pre_ffn, g_post_ffn = map(vec, (g_pre_mix, g_post_mix, g_pre_ffn,
                                                             g_post_ffn))
    half = jnp.concatenate([jnp.ones((D_FF,), F32), jnp.full((D_FF,), 0.5, F32)])
    b_gate, conv_b, conv_w = vec(b_gate), vec(conv_b * half), conv_w.astype(F32) * half
    assert rpb_na.shape[1:] == (NA_HEADS, 2 * NA_KH - 1, 2 * NA_KW - 1)
    rp = jnp.pad(rpb_na.astype(F32) * LOG2E, ((0, 0), (0, 0), (0, 0),
                                      (NA_RPB_PAD, LANES - NA_RPB_PAD - (2 * NA_KW - 1))))
    kv = _memkv(mem, mem_norm_g.reshape(1, -1).astype(F32), bf(w_mem_kv))

    x2 = x.reshape(b * s, d)
    for l in range(depth):
        if l == 0:
            *proj, w_up, w_down, w_out, w_in_rest = (*_inproj(
                x2, g_pre_mix, w_in_first, b_gate, 0, 0, b, s, IN_TM_FIRST, late_casts),
                *([None] * (depth == 1)))
            w_up = w_up.reshape(depth, D_MODEL, 2 * D_FF)
            w_down = w_down.reshape(depth, D_FF, D_MODEL)
            w_out = w_out.reshape(depth, D_MODEL, D_MODEL)
            if depth > 1:
                w_in_rest = w_in_rest.reshape(depth - 1, D_MODEL, D_IN)
        else:
            proj = _inproj(x2, g_pre_mix, w_in_rest, b_gate, l - 1, l, b, s, IN_TM)
        qa, ka, va, b0, b1, b2, qm, gates = proj
        oa = _na_attention(qa, ka, va, rp, l, b, s)
        ob, lse = zip(*(_dil_attention(q, grp) for grp, q in enumerate((b0, b1, b2))))
        x2 = _merge(x2, oa, ob, lse, qm, gates, kv, w_br_a, w_br_b, w_br_m, w_out, g_post_mix, l, s)
        x2 = _ffn(x2, g_pre_ffn, w_up, conv_w, conv_b, w_down, g_post_ffn, l, s)
    return x2.reshape(b, s, d)
```

```python
import functools

import jax
import jax.numpy as jnp
from jax import lax
from jax.experimental import pallas as pl
from jax.experimental.pallas import tpu as pltpu

D_MODEL = 1024
GRID_W = 64
N_MEM = 256
HEAD_DIM = 64
NA_HEADS = 6
NA_KH = 8
NA_KW = 16
DIL_PAIRS = ((128, 1), (512, 4), (2048, 16))
DIL_HEADS = 6
MEM_HEADS = 4
D_A = 384
D_B = 384
D_B_OUT = 128
D_M = 256
N_BRANCH = 3
D_IN = 3 * D_A + 3 * D_B + D_M + N_BRANCH * D_MODEL
D_FF = 2816
RMS_EPS = 1e-6
NEG_INF = -1e30

LANES = 128
SUB = 8
SUB_BF16 = 16
VMEM_LIMIT = 56 * 1024 * 1024
IN_TM = 1024
IN_TM_FIRST = 512
MG_TM = 1024
FF_TM = 1024

F32 = jnp.float32
BF16 = jnp.bfloat16

LOG2E = 1.4426950408889634
LN2 = 0.6931471805599453
Q_SCALE = HEAD_DIM ** -0.5 * LOG2E


def _rms(x, g):
    return x * lax.rsqrt(jnp.mean(x * x, axis=-1, keepdims=True) + RMS_EPS) * g


def _dot(a, b):
    return jnp.dot(a, b, preferred_element_type=F32)


def _dot_nt(a, b):
    return lax.dot_general(a, b, (((1,), (1,)), ((), ())), preferred_element_type=F32)


def _layer_param(shape, layer):
    zeros = (0,) * len(shape)
    return pl.BlockSpec((1,) + tuple(shape), lambda *_: (layer,) + zeros,
                        pipeline_mode=pl.Buffered(1))


def _params(n_axes=1):
    return pltpu.CompilerParams(dimension_semantics=("arbitrary",) * n_axes,
                                vmem_limit_bytes=VMEM_LIMIT)


def _memkv_kernel(mem_ref, g_ref, w_ref, o_ref):
    mn = _rms(mem_ref[0], g_ref[...]).astype(BF16)
    o_ref[0, 0] = _dot(mn, w_ref[0]).astype(BF16)


def _memkv(mem, g, w_kv):
    depth = w_kv.shape[0]
    b = mem.shape[0]
    return pl.pallas_call(
        _memkv_kernel,
        grid=(depth, b),
        in_specs=[
            pl.BlockSpec((1, N_MEM, D_MODEL), lambda l, i: (i, 0, 0)),
            pl.BlockSpec((1, D_MODEL), lambda l, i: (0, 0)),
            pl.BlockSpec((1, D_MODEL, 2 * D_M), lambda l, i: (l, 0, 0)),
        ],
        out_specs=pl.BlockSpec((1, 1, N_MEM, 2 * D_M), lambda l, i: (l, i, 0, 0)),
        out_shape=jax.ShapeDtypeStruct((depth, b, N_MEM, 2 * D_M), BF16),
        compiler_params=_params(2),
        name="memkv",
    )(mem, g, w_kv)


IN_CHUNK = 512
_A_SLABS = D_A // LANES
_M_SLABS = D_M // LANES
_N_GROUPS = len(DIL_PAIRS)


def _inproj_kernel(*refs, n_casts):
    x_ref, g_ref, w_ref, bg_ref = refs[:4]
    cast_in = refs[4:4 + n_casts]
    qa_ref, ka_ref, va_ref, b0_ref, b1_ref, b2_ref, qm_ref, gate_ref = refs[4 + n_casts:12 + n_casts]
    cast_out = refs[12 + n_casts:12 + 2 * n_casts]
    stage_ref = refs[-1]
    tm = x_ref.shape[0]
    x = x_ref[...]
    h = (x * g_ref[0]).astype(BF16)
    inv = jnp.broadcast_to(lax.rsqrt(jnp.mean(x * x, axis=-1, keepdims=True) + RMS_EPS),
                           (tm, LANES))
    inv_q = inv * Q_SCALE
    b_refs = (b0_ref, b1_ref, b2_ref)

    def put_dilated(which, grp, val):
        d = DIL_PAIRS[grp][1]
        cs = slice(which * LANES, (which + 1) * LANES)
        if d == 1:
            b_refs[grp][0, 0, :, cs] = val.astype(BF16)
            return
        stage = stage_ref.at[(grp - 1) * 3 + which]
        stage[...] = val
        for r in range(d):
            b_refs[grp][0, r, :, cs] = stage[pl.ds(r, tm // d, stride=d), :].astype(BF16)

    def put(slab, val):
        s = slab
        if s < 3 * _A_SLABS:
            which, j = divmod(s, _A_SLABS)
            ref = (qa_ref, ka_ref, va_ref)[which]
            ref[:, j * LANES:(j + 1) * LANES] = (val * (inv_q if which == 0 else inv)).astype(BF16)
            return
        s -= 3 * _A_SLABS
        if s < 3 * _N_GROUPS:
            which, grp = divmod(s, _N_GROUPS)
            put_dilated(which, grp, val * (inv_q if which == 0 else inv))
            return
        s -= 3 * _N_GROUPS
        if s < _M_SLABS:
            qm_ref[:, s * LANES:(s + 1) * LANES] = (val * inv_q).astype(BF16)
            return
        s -= _M_SLABS
        gate = jax.nn.sigmoid(val * inv + bg_ref[0, :, s * LANES:(s + 1) * LANES])
        gate_ref[:, s * LANES:(s + 1) * LANES] = gate.astype(BF16)

    per = IN_CHUNK // LANES
    for c in range(D_IN // IN_CHUNK):
        r = _dot(h, w_ref[0, :, c * IN_CHUNK:(c + 1) * IN_CHUNK])
        for j in range(per):
            put(c * per + j, r[:, j * LANES:(j + 1) * LANES])

    for src, dst in zip(cast_in, cast_out):
        dst[...] = src[...].astype(BF16)


def _inproj(x2, g, w_in, b_gate, layer, g_layer, b, s, tm, casts=()):
    t = x2.shape[0]
    assert s % tm == 0
    tiles_per_seq = s // tm
    steps = t // tm
    row = lambda i: (i, 0)
    outs = [jax.ShapeDtypeStruct((t, D_A), BF16)] * 3
    out_specs = [pl.BlockSpec((tm, D_A), row)] * 3
    for _, d in DIL_PAIRS:
        assert tm % (d * SUB_BF16) == 0
        outs.append(jax.ShapeDtypeStruct((b, d, s // d, 3 * LANES), BF16))
        out_specs.append(pl.BlockSpec((1, d, tm // d, 3 * LANES),
                                      lambda i: (i // tiles_per_seq, 0, i % tiles_per_seq, 0)))
    outs += [jax.ShapeDtypeStruct((t, D_M), BF16),
             jax.ShapeDtypeStruct((t, N_BRANCH * D_MODEL), BF16)]
    out_specs += [pl.BlockSpec((tm, D_M), row),
                  pl.BlockSpec((tm, N_BRANCH * D_MODEL), row)]
    cast_in_specs, cast_out_specs = [], []
    for a, skip in casts:
        rows, cols = a.shape
        blk = (rows - skip) // steps
        assert (rows - skip) % (steps * SUB_BF16) == 0 and skip % blk == 0
        cast_in_specs.append(pl.BlockSpec((blk, cols), lambda i, o=skip // blk: (o + i, 0)))
        cast_out_specs.append(pl.BlockSpec((blk, cols), row))
        outs.append(jax.ShapeDtypeStruct((rows - skip, cols), BF16))
    return pl.pallas_call(
        functools.partial(_inproj_kernel, n_casts=len(casts)),
        grid=(steps,),
        in_specs=[
            pl.BlockSpec((tm, D_MODEL), row),
            _layer_param((1, D_MODEL), g_layer),
            _layer_param((D_MODEL, D_IN), layer),
            _layer_param((1, N_BRANCH * D_MODEL), g_layer),
            *cast_in_specs,
        ],
        out_specs=out_specs + cast_out_specs,
        out_shape=outs,
        scratch_shapes=[pltpu.VMEM((3 * (_N_GROUPS - 1), tm, LANES), F32)],
        compiler_params=_params(),
        name="inproj",
    )(x2, g, w_in, b_gate, *(a for a, _ in casts))


def _pair_scores(q2, k2):
    lane = lax.broadcasted_iota(jnp.int32, q2.shape, 1)
    zero = jnp.zeros_like(q2)
    qs = jnp.concatenate([jnp.where(lane < HEAD_DIM, q2, zero),
                          jnp.where(lane < HEAD_DIM, zero, q2)], axis=0)
    return _dot_nt(qs, k2)


def _softmax_parts(s):
    m = jnp.max(s, axis=-1, keepdims=True)
    return m, jnp.exp2(s - m).astype(BF16)


def _pv_and_den(p, v2):
    r = _dot(p, jnp.concatenate([v2, jnp.ones_like(v2)], axis=1))
    return r[:, :LANES], r[:, LANES:]


def _unstack_heads(a):
    m_rows = a.shape[0] // 2
    lane = lax.broadcasted_iota(jnp.int32, (m_rows, LANES), 1)
    return jnp.where(lane < HEAD_DIM, a[:m_rows], a[m_rows:])


NA_ROWS_PER_STEP = 16
NA_GROUP = 16
NA_BAND = NA_KH * GRID_W


NA_RPB_PAD = GRID_W - NA_KW


def _na_build_bias(rp_ref, toe_ref, tab_ref):
    lane = lax.broadcasted_iota(jnp.int32, (GRID_W, LANES), 1)
    q = lax.broadcasted_iota(jnp.int32, (GRID_W, LANES), 0)
    c = lane % GRID_W
    c0 = jnp.clip(q - NA_KW // 2, 0, GRID_W - NA_KW)
    valid = (c >= c0) & (c < c0 + NA_KW)
    base_lo, base_hi = LANES - (GRID_W - 1), LANES - (GRID_W - 1) - GRID_W

    def toe_body(ro, carry):
        for h in range(NA_HEADS):
            row = jnp.broadcast_to(rp_ref[0, h, pl.ds(ro, 1), :], (GRID_W, LANES))
            lo = pltpu.roll(row, base_lo, 1, stride=1, stride_axis=0)
            hi = pltpu.roll(row, base_hi, 1, stride=1, stride_axis=0)
            toe_ref[h, ro] = jnp.where(valid, jnp.where(lane < GRID_W, lo, hi), NEG_INF)
        return carry

    lax.fori_loop(0, 2 * NA_KH - 1, toe_body, 0)

    def tab_body(dl, carry):
        for h in range(NA_HEADS):
            rows_ = slice((h % 2) * GRID_W, (h % 2 + 1) * GRID_W)
            for g in range(NA_KH // 2):
                even, odd = toe_ref[h, dl + 2 * g], toe_ref[h, dl + 2 * g + 1]
                tab_ref[h // 2, dl, rows_, g * LANES:(g + 1) * LANES] = \
                    jnp.where(lane < GRID_W, even, odd)
        return carry

    lax.fori_loop(0, NA_KH, tab_body, 0)


def _na_kernel(q_ref, k_ref, v_ref, rp_ref, o_ref, toe_ref, tab_ref, *, rows):
    rb = pl.program_id(1)

    @pl.when((pl.program_id(0) == 0) & (rb == 0))
    def _():
        _na_build_bias(rp_ref, toe_ref, tab_ref)

    def group_body(gi, carry):
        chains = []
        for lg in range(NA_GROUP):
            lr = gi * NA_GROUP + lg
            i = rb * NA_ROWS_PER_STEP + lr
            r0 = jnp.clip(i - NA_KH // 2, 0, rows - NA_KH)
            ks = pl.multiple_of(r0 * GRID_W, GRID_W)
            qs = pl.multiple_of(lr * GRID_W, GRID_W)
            for hp in range(NA_HEADS // 2):
                chains.append((qs, ks, r0 - i + (NA_KH - 1), hp,
                               slice(hp * LANES, (hp + 1) * LANES)))
        ss = [_pair_scores(q_ref[0, pl.ds(qs, GRID_W), cs], k_ref[0, pl.ds(ks, NA_BAND), cs])
              for qs, ks, dl, hp, cs in chains]
        parts = [_softmax_parts(s + tab_ref[hp, dl])
                 for s, (qs, ks, dl, hp, cs) in zip(ss, chains)]
        for (_, p), (qs, ks, dl, hp, cs) in zip(parts, chains):
            pv, den = _pv_and_den(p, v_ref[0, pl.ds(ks, NA_BAND), cs])
            o_ref[0, pl.ds(qs, GRID_W), cs] = _unstack_heads(pv / den).astype(BF16)
        return carry

    lax.fori_loop(0, NA_ROWS_PER_STEP // NA_GROUP, group_body, 0)


def _na_attention(q, k, v, rp, layer, b, s):
    rows = s // GRID_W
    qblk = NA_ROWS_PER_STEP * GRID_W
    q3, k3, v3 = (a.reshape(b, s, D_A) for a in (q, k, v))
    out = pl.pallas_call(
        functools.partial(_na_kernel, rows=rows),
        grid=(b, rows // NA_ROWS_PER_STEP),
        in_specs=[
            pl.BlockSpec((1, qblk, D_A), lambda i, j: (i, j, 0)),
            pl.BlockSpec((1, s, D_A), lambda i, j: (i, 0, 0)),
            pl.BlockSpec((1, s, D_A), lambda i, j: (i, 0, 0)),
            _layer_param(rp.shape[1:], layer),
        ],
        out_specs=pl.BlockSpec((1, qblk, D_A), lambda i, j: (i, j, 0)),
        out_shape=jax.ShapeDtypeStruct((b, s, D_A), BF16),
        scratch_shapes=[pltpu.VMEM((NA_HEADS, 2 * NA_KH - 1, GRID_W, LANES), F32),
                        pltpu.VMEM((NA_HEADS // 2, NA_KH, 2 * GRID_W, NA_BAND), F32)],
        compiler_params=_params(2),
        name="na_attn",
    )(q3, k3, v3, rp)
    return out.reshape(b * s, D_A)


DIL_QB = 128
DIL_HALF = 64
DIL_KB = DIL_QB + 2 * DIL_HALF
DIL_GROUP = 32
DIL_KEY_SHIFTS = (0, -1, -2)


def _alibi_slope(head):
    return 2.0 ** (-8.0 * (head + 1) / DIL_HEADS)


def _dil_kernel(x_ref, o_ref, lse_ref, bias_ref, *, length, dilation, slopes, residues):
    nblk = length // DIL_QB

    @pl.when((pl.program_id(0) == 0) & (pl.program_id(1) == 0))
    def _():
        rel0 = (lax.broadcasted_iota(jnp.int32, (DIL_QB, DIL_KB), 1)
                - lax.broadcasted_iota(jnp.int32, (DIL_QB, DIL_KB), 0))
        for c, shift in enumerate(DIL_KEY_SHIFTS):
            dist = jnp.abs(rel0 + shift * DIL_HALF)
            dist_f = (dist * dilation).astype(F32)
            for hh in range(2):
                bias = jnp.where(dist <= DIL_HALF, (-slopes[hh] * LOG2E) * dist_f, NEG_INF)
                bias_ref[c, hh * DIL_QB:(hh + 1) * DIL_QB, :] = bias

    def group(chains):
        geo = []
        for r, bi in chains:
            q0 = bi * DIL_QB
            if isinstance(bi, int):
                ks = min(max(q0 - DIL_HALF, 0), length - DIL_KB)
            else:
                q0 = pl.multiple_of(q0, DIL_QB)
                ks = pl.multiple_of(jnp.clip(q0 - DIL_HALF, 0, length - DIL_KB), DIL_HALF)
            geo.append((r, q0, ks, (q0 - ks) // DIL_HALF))
        ss = [_pair_scores(x_ref[0, r, pl.ds(q0, DIL_QB), 0:LANES],
                           x_ref[0, r, pl.ds(ks, DIL_KB), LANES:2 * LANES])
              for r, q0, ks, case in geo]
        parts = [_softmax_parts(s + bias_ref[case]) for s, (r, q0, ks, case) in zip(ss, geo)]
        for (m, p), (r, q0, ks, case) in zip(parts, geo):
            pv, den = _pv_and_den(p, x_ref[0, r, pl.ds(ks, DIL_KB), 2 * LANES:3 * LANES])
            o_ref[0, r, pl.ds(q0, DIL_QB), :] = _unstack_heads(pv / den).astype(BF16)
            lse_ref[0, r, pl.ds(q0, DIL_QB), :] = _unstack_heads(m * LN2 + jnp.log(den))

    if residues * nblk <= DIL_GROUP:
        group([(r, bi) for r in range(residues) for bi in range(nblk)])
    else:
        assert nblk % DIL_GROUP == 0
        for r in range(residues):
            def body(gi, carry, r=r):
                group([(r, gi * DIL_GROUP + j) for j in range(DIL_GROUP)])
                return carry
            lax.fori_loop(0, nblk // DIL_GROUP, body, 0)


def _dil_attention(qkv, grp):
    window, dilation = DIL_PAIRS[grp]
    assert window // 2 // dilation == DIL_HALF
    b, _, length, width = qkv.shape
    assert length % DIL_QB == 0 and length >= DIL_KB
    nblk = length // DIL_QB
    residues = min(dilation, max(1, DIL_GROUP // nblk))
    assert dilation % residues == 0
    slopes = tuple(_alibi_slope(2 * grp + hh) for hh in range(2))
    blk = lambda w: pl.BlockSpec((1, residues, length, w), lambda i, r: (i, r, 0, 0))
    return pl.pallas_call(
        functools.partial(_dil_kernel, length=length, dilation=dilation, slopes=slopes,
                          residues=residues),
        grid=(b, dilation // residues),
        in_specs=[blk(width)],
        out_specs=[blk(LANES), blk(LANES)],
        out_shape=[jax.ShapeDtypeStruct((b, dilation, length, LANES), BF16),
                   jax.ShapeDtypeStruct((b, dilation, length, LANES), F32)],
        scratch_shapes=[pltpu.VMEM((len(DIL_KEY_SHIFTS), 2 * DIL_QB, DIL_KB), F32)],
        compiler_params=_params(2),
        name=f"dil_attn_{grp}",
    )(qkv)


def _merge_kernel(x_ref, oa_ref, o0_ref, o1_ref, o2_ref, l0_ref, l1_ref, l2_ref, qm_ref, gate_ref,
                  kv_ref, wa_ref, wb_ref, wm_ref, wo_ref, g_ref, out_ref, stage_ref):
    pairs = range(MEM_HEADS // 2)
    ss = [_pair_scores(qm_ref[:, hp * LANES:(hp + 1) * LANES],
                       kv_ref[0, 0, :, hp * LANES:(hp + 1) * LANES]) for hp in pairs]
    proj_a = _dot(oa_ref[...], wa_ref[0])

    def natural_order(ref, grp, slot):
        d = DIL_PAIRS[grp][1]
        if d == 1:
            return ref[0, 0].astype(F32)
        stage = stage_ref.at[slot]
        for r in range(d):
            stage[pl.ds(r, MG_TM // d, stride=d), :] = ref[0, r].astype(F32)
        return stage[...]

    o_refs, l_refs = (o0_ref, o1_ref, o2_ref), (l0_ref, l1_ref, l2_ref)
    os_ = [natural_order(o_refs[g], g, 2 * (g - 1)) for g in range(_N_GROUPS)]
    ls = [natural_order(l_refs[g], g, 2 * (g - 1) + 1) for g in range(_N_GROUPS)]
    mx = jnp.maximum(jnp.maximum(ls[0], ls[1]), ls[2])
    es = [jnp.exp(l - mx) for l in ls]
    inv = 1.0 / (es[0] + es[1] + es[2])
    o_b = ((es[0] * inv) * os_[0] + (es[1] * inv) * os_[1] + (es[2] * inv) * os_[2]).astype(BF16)
    proj_b = _dot(o_b, wb_ref[0])

    om = []
    for hp, s in zip(pairs, ss):
        _, p = _softmax_parts(s)
        pv, den = _pv_and_den(p, kv_ref[0, 0, :, D_M + hp * LANES:D_M + (hp + 1) * LANES])
        om.append(_unstack_heads(pv / den).astype(BF16))
    o_m = jnp.concatenate(om, axis=-1)

    merged = (gate_ref[:, 0:D_MODEL].astype(F32) * proj_a
              + gate_ref[:, D_MODEL:2 * D_MODEL].astype(F32) * proj_b
              + gate_ref[:, 2 * D_MODEL:3 * D_MODEL].astype(F32) * _dot(o_m, wm_ref[0]))
    merged = merged.astype(BF16)
    for r0 in range(0, MG_TM, MG_OUT_ROWS):
        rows = slice(r0, r0 + MG_OUT_ROWS)
        y = _dot(merged[rows], wo_ref[0])
        out_ref[rows, :] = x_ref[rows, :] + _rms(y, g_ref[0])


def _merge(x2, oa, ob, lse, qm, gates, kv, wa, wb, wm, wo, g, layer, s):
    t = x2.shape[0]
    tiles_per_seq = s // MG_TM
    row = lambda i: (i, 0)
    dil = [pl.BlockSpec((1, d, MG_TM // d, LANES),
                        lambda i: (i // tiles_per_seq, 0, i % tiles_per_seq, 0))
           for _, d in DIL_PAIRS]
    in_specs = [
        pl.BlockSpec((MG_TM, D_MODEL), row),
        pl.BlockSpec((MG_TM, D_A), row),
        *dil, *dil,
        pl.BlockSpec((MG_TM, D_M), row),
        pl.BlockSpec((MG_TM, N_BRANCH * D_MODEL), row),
        pl.BlockSpec((1, 1, N_MEM, 2 * D_M), lambda i: (layer, i // tiles_per_seq, 0, 0)),
        _layer_param((D_A, D_MODEL), layer), _layer_param((D_B_OUT, D_MODEL), layer),
        _layer_param((D_M, D_MODEL), layer), _layer_param((D_MODEL, D_MODEL), layer),
        _layer_param((1, D_MODEL), layer),
    ]
    return pl.pallas_call(
        _merge_kernel,
        grid=(t // MG_TM,),
        in_specs=in_specs,
        out_specs=pl.BlockSpec((MG_TM, D_MODEL), row),
        out_shape=jax.ShapeDtypeStruct((t, D_MODEL), F32),
        scratch_shapes=[pltpu.VMEM((2 * (_N_GROUPS - 1), MG_TM, LANES), F32)],
        compiler_params=_params(),
        name="merge",
    )(x2, oa, *ob, *lse, qm, gates, kv, wa, wb, wm, wo, g)


FF_CHUNK = 256
MG_OUT_ROWS = 256
GELU_C0 = 0.7978845608028654
GELU_C1 = GELU_C0 * 0.044715


def _gelu_gate(a, half_b):
    t = jnp.tanh(a * (GELU_C0 + GELU_C1 * (a * a)))
    hb = a * half_b
    return hb + hb * t


def _ffn_kernel(xm_ref, xp_ref, xn_ref, g1_ref, wup_ref, cw_ref, cb_ref, wdn_ref, g2_ref, out_ref,
                hext_ref, f_ref, *, tiles_per_seq):
    i = pl.program_id(0)
    pos = i % tiles_per_seq
    g1 = g1_ref[0]
    x = xm_ref[...]
    hext_ref[0:FF_TM, :] = _rms(x, g1).astype(BF16)
    hn = jnp.where(pos == tiles_per_seq - 1, 0.0, _rms(xn_ref[...], g1))
    hp = jnp.where(pos == 0, 0.0, _rms(xp_ref[...], g1))
    hext_ref[FF_TM:FF_TM + 2 * SUB, :] = jnp.concatenate([hn, hp], axis=0).astype(BF16)
    hext = hext_ref[...]
    ext = FF_TM + 2 * SUB

    def conv(u, col):
        cs = slice(col, col + FF_CHUNK)
        before = pltpu.roll(u, 1, 0)[0:FF_TM]
        after = pltpu.roll(u, ext - 1, 0)[0:FF_TM]
        y = cb_ref[0, :, cs] + before * cw_ref[0, 0:1, cs]
        y = y + u[0:FF_TM] * cw_ref[0, 1:2, cs]
        return y + after * cw_ref[0, 2:3, cs]

    for c in range(D_FF // FF_CHUNK):
        ca, cb = c * FF_CHUNK, D_FF + c * FF_CHUNK
        ua = conv(_dot(hext, wup_ref[0, :, ca:ca + FF_CHUNK]), ca)
        ub = conv(_dot(hext, wup_ref[0, :, cb:cb + FF_CHUNK]), cb)
        f_ref[:, ca:ca + FF_CHUNK] = _gelu_gate(ua, ub).astype(BF16)

    y = _dot(f_ref[...], wdn_ref[0])
    out_ref[...] = x + _rms(y, g2_ref[0])


def _ffn(x2, g1, w_up, cw, cb, w_dn, g2, layer, s):
    t = x2.shape[0]
    assert s % FF_TM == 0
    tiles_per_seq = s // FF_TM
    per = FF_TM // SUB
    nh = t // SUB
    row = lambda i: (i, 0)
    return pl.pallas_call(
        functools.partial(_ffn_kernel, tiles_per_seq=tiles_per_seq),
        grid=(t // FF_TM,),
        in_specs=[
            pl.BlockSpec((FF_TM, D_MODEL), row),
            pl.BlockSpec((SUB, D_MODEL), lambda i: (jnp.maximum(i * per - 1, 0), 0)),
            pl.BlockSpec((SUB, D_MODEL), lambda i: (jnp.minimum((i + 1) * per, nh - 1), 0)),
            _layer_param((1, D_MODEL), layer),
            _layer_param((D_MODEL, 2 * D_FF), layer),
            _layer_param((3, 2 * D_FF), layer),
            _layer_param((1, 2 * D_FF), layer),
            _layer_param((D_FF, D_MODEL), layer),
            _layer_param((1, D_MODEL), layer),
        ],
        out_specs=pl.BlockSpec((FF_TM, D_MODEL), row),
        out_shape=jax.ShapeDtypeStruct((t, D_MODEL), F32),
        scratch_shapes=[pltpu.VMEM((FF_TM + 2 * SUB, D_MODEL), BF16),
                        pltpu.VMEM((FF_TM, D_FF), BF16)],
        compiler_params=_params(),
        name="ffn",
    )(x2, x2, x2, g1, w_up, cw, cb, w_dn, g2)


def kernel(x, mem, mem_norm_g, g_pre_mix, w_in, rpb_na, w_mem_kv, b_gate, w_br_a, w_br_b, w_br_m,
           w_out, g_post_mix, g_pre_ffn, w_up, conv_w, conv_b, w_down, g_post_ffn):
    b, s, d = x.shape
    depth = w_in.shape[0]
    assert d == D_MODEL and s % GRID_W == 0 and s % MG_TM == 0
    rows = s // GRID_W
    assert rows >= NA_KH and rows % NA_ROWS_PER_STEP == 0

    bf = lambda a: a.astype(BF16)
    vec = lambda a: a.reshape(depth, 1, -1).astype(F32)

    w_in_first = bf(w_in[:1])
    late_casts = [(w_up.reshape(depth * D_MODEL, 2 * D_FF), 0),
                  (w_down.reshape(depth * D_FF, D_MODEL), 0),
                  (w_out.reshape(depth * D_MODEL, D_MODEL), 0)]
    if depth > 1:
        late_casts.append((w_in.reshape(depth * D_MODEL, D_IN), D_MODEL))
    w_br_a, w_br_b, w_br_m = bf(w_br_a), bf(w_br_b), bf(w_br_m)
    g_pre_mix, g_post_mix, g_pre_ffn, g_post_ffn = map(vec, (g_pre_mix, g_post_mix, g_pre_ffn,
                                                             g_post_ffn))
    half = jnp.concatenate([jnp.ones((D_FF,), F32), jnp.full((D_FF,), 0.5, F32)])
    b_gate, conv_b, conv_w = vec(b_gate), vec(conv_b * half), conv_w.astype(F32) * half
    assert rpb_na.shape[1:] == (NA_HEADS, 2 * NA_KH - 1, 2 * NA_KW - 1)
    rp = jnp.pad(rpb_na.astype(F32) * LOG2E, ((0, 0), (0, 0), (0, 0),
                                      (NA_RPB_PAD, LANES - NA_RPB_PAD - (2 * NA_KW - 1))))
    kv = _memkv(mem, mem_norm_g.reshape(1, -1).astype(F32), bf(w_mem_kv))

    x2 = x.reshape(b * s, d)
    for l in range(depth):
        if l == 0:
            *proj, w_up, w_down, w_out, w_in_rest = (*_inproj(
                x2, g_pre_mix, w_in_first, b_gate, 0, 0, b, s, IN_TM_FIRST, late_casts),
                *([None] * (depth == 1)))
            w_up = w_up.reshape(depth, D_MODEL, 2 * D_FF)
            w_down = w_down.reshape(depth, D_FF, D_MODEL)
            w_out = w_out.reshape(depth, D_MODEL, D_MODEL)
            if depth > 1:
                w_in_rest = w_in_rest.reshape(depth - 1, D_MODEL, D_IN)
        else:
            proj = _inproj(x2, g_pre_mix, w_in_rest, b_gate, l - 1, l, b, s, IN_TM)
        qa, ka, va, b0, b1, b2, qm, gates = proj
        oa = _na_attention(qa, ka, va, rp, l, b, s)
        ob, lse = zip(*(_dil_attention(q, grp) for grp, q in enumerate((b0, b1, b2))))
        x2 = _merge(x2, oa, ob, lse, qm, gates, kv, w_br_a, w_br_b, w_br_m, w_out, g_post_mix, l, s)
        x2 = _ffn(x2, g_pre_ffn, w_up, conv_w, conv_b, w_down, g_post_ffn, l, s)
    return x2.reshape(b, s, d)
```

```python
import functools

import jax
import jax.numpy as jnp
from jax import lax
from jax.experimental import pallas as pl
from jax.experimental.pallas import tpu as pltpu

D_MODEL = 1024
GRID_W = 64
N_MEM = 256
HEAD_DIM = 64
NA_HEADS = 6
NA_KH = 8
NA_KW = 16
DIL_PAIRS = ((128, 1), (512, 4), (2048, 16))
DIL_HEADS = 6
MEM_HEADS = 4
D_A = 384
D_B = 384
D_B_OUT = 128
D_M = 256
N_BRANCH = 3
D_IN = 3 * D_A + 3 * D_B + D_M + N_BRANCH * D_MODEL
D_FF = 2816
RMS_EPS = 1e-6
NEG_INF = -1e30

LANES = 128
SUB = 8
SUB_BF16 = 16
VMEM_LIMIT = 56 * 1024 * 1024
IN_TM = 1024
IN_TM_FIRST = 512
MG_TM = 1024
MG_SUB = 512
FF_TM = 1024

F32 = jnp.float32
BF16 = jnp.bfloat16

LOG2E = 1.4426950408889634
LN2 = 0.6931471805599453
Q_SCALE = HEAD_DIM ** -0.5 * LOG2E


def _rms(x, g):
    return x * lax.rsqrt(jnp.mean(x * x, axis=-1, keepdims=True) + RMS_EPS) * g


def _dot(a, b):
    return jnp.dot(a, b, preferred_element_type=F32)


def _dot_nt(a, b):
    return lax.dot_general(a, b, (((1,), (1,)), ((), ())), preferred_element_type=F32)


def _layer_param(shape, layer):
    zeros = (0,) * len(shape)
    return pl.BlockSpec((1,) + tuple(shape), lambda *_: (layer,) + zeros,
                        pipeline_mode=pl.Buffered(1))


def _params(n_axes=1):
    return pltpu.CompilerParams(dimension_semantics=("arbitrary",) * n_axes,
                                vmem_limit_bytes=VMEM_LIMIT)


def _memkv_kernel(mem_ref, g_ref, w_ref, o_ref):
    mn = _rms(mem_ref[0], g_ref[...]).astype(BF16)
    o_ref[0, 0] = _dot(mn, w_ref[0]).astype(BF16)


def _memkv(mem, g, w_kv):
    depth = w_kv.shape[0]
    b = mem.shape[0]
    return pl.pallas_call(
        _memkv_kernel,
        grid=(depth, b),
        in_specs=[
            pl.BlockSpec((1, N_MEM, D_MODEL), lambda l, i: (i, 0, 0)),
            pl.BlockSpec((1, D_MODEL), lambda l, i: (0, 0)),
            pl.BlockSpec((1, D_MODEL, 2 * D_M), lambda l, i: (l, 0, 0)),
        ],
        out_specs=pl.BlockSpec((1, 1, N_MEM, 2 * D_M), lambda l, i: (l, i, 0, 0)),
        out_shape=jax.ShapeDtypeStruct((depth, b, N_MEM, 2 * D_M), BF16),
        compiler_params=_params(2),
        name="memkv",
    )(mem, g, w_kv)


IN_CHUNK = 512
_A_SLABS = D_A // LANES
_M_SLABS = D_M // LANES
_N_GROUPS = len(DIL_PAIRS)


def _inproj_kernel(*refs, n_casts):
    x_ref, g_ref, w_ref, bg_ref = refs[:4]
    cast_in = refs[4:4 + n_casts]
    qa_ref, ka_ref, va_ref, b0_ref, b1_ref, b2_ref, qm_ref, gate_ref = refs[4 + n_casts:12 + n_casts]
    cast_out = refs[12 + n_casts:12 + 2 * n_casts]
    stage_ref = refs[-1]
    tm = x_ref.shape[0]
    x = x_ref[...]
    h = (x * g_ref[0]).astype(BF16)
    inv = jnp.broadcast_to(lax.rsqrt(jnp.mean(x * x, axis=-1, keepdims=True) + RMS_EPS),
                           (tm, LANES))
    inv_q = inv * Q_SCALE
    b_refs = (b0_ref, b1_ref, b2_ref)

    def put_dilated(which, grp, val):
        d = DIL_PAIRS[grp][1]
        cs = slice(which * LANES, (which + 1) * LANES)
        if d == 1:
            b_refs[grp][0, 0, :, cs] = val.astype(BF16)
            return
        stage = stage_ref.at[(grp - 1) * 3 + which]
        stage[...] = val
        for r in range(d):
            b_refs[grp][0, r, :, cs] = stage[pl.ds(r, tm // d, stride=d), :].astype(BF16)

    def put(slab, val):
        s = slab
        if s < 3 * _A_SLABS:
            which, j = divmod(s, _A_SLABS)
            ref = (qa_ref, ka_ref, va_ref)[which]
            ref[:, j * LANES:(j + 1) * LANES] = (val * (inv_q if which == 0 else inv)).astype(BF16)
            return
        s -= 3 * _A_SLABS
        if s < 3 * _N_GROUPS:
            which, grp = divmod(s, _N_GROUPS)
            put_dilated(which, grp, val * (inv_q if which == 0 else inv))
            return
        s -= 3 * _N_GROUPS
        if s < _M_SLABS:
            qm_ref[:, s * LANES:(s + 1) * LANES] = (val * inv_q).astype(BF16)
            return
        s -= _M_SLABS
        gate = jax.nn.sigmoid(val * inv + bg_ref[0, :, s * LANES:(s + 1) * LANES])
        gate_ref[:, s * LANES:(s + 1) * LANES] = gate.astype(BF16)

    per = IN_CHUNK // LANES
    for c in reversed(range(D_IN // IN_CHUNK)):
        r = _dot(h, w_ref[0, :, c * IN_CHUNK:(c + 1) * IN_CHUNK])
        for j in range(per):
            put(c * per + j, r[:, j * LANES:(j + 1) * LANES])

    for src, dst in zip(cast_in, cast_out):
        dst[...] = src[...].astype(BF16)


def _inproj(x2, g, w_in, b_gate, layer, g_layer, b, s, tm, casts=()):
    t = x2.shape[0]
    assert s % tm == 0
    tiles_per_seq = s // tm
    steps = t // tm
    row = lambda i: (i, 0)
    outs = [jax.ShapeDtypeStruct((t, D_A), BF16)] * 3
    out_specs = [pl.BlockSpec((tm, D_A), row)] * 3
    for _, d in DIL_PAIRS:
        assert tm % (d * SUB_BF16) == 0
        outs.append(jax.ShapeDtypeStruct((b, d, s // d, 3 * LANES), BF16))
        out_specs.append(pl.BlockSpec((1, d, tm // d, 3 * LANES),
                                      lambda i: (i // tiles_per_seq, 0, i % tiles_per_seq, 0)))
    outs += [jax.ShapeDtypeStruct((t, D_M), BF16),
             jax.ShapeDtypeStruct((t, N_BRANCH * D_MODEL), BF16)]
    out_specs += [pl.BlockSpec((tm, D_M), row),
                  pl.BlockSpec((tm, N_BRANCH * D_MODEL), row)]
    cast_in_specs, cast_out_specs = [], []
    for a, skip in casts:
        rows, cols = a.shape
        blk = (rows - skip) // steps
        assert (rows - skip) % (steps * SUB_BF16) == 0 and skip % blk == 0
        cast_in_specs.append(pl.BlockSpec((blk, cols), lambda i, o=skip // blk: (o + i, 0)))
        cast_out_specs.append(pl.BlockSpec((blk, cols), row))
        outs.append(jax.ShapeDtypeStruct((rows - skip, cols), BF16))
    return pl.pallas_call(
        functools.partial(_inproj_kernel, n_casts=len(casts)),
        grid=(steps,),
        in_specs=[
            pl.BlockSpec((tm, D_MODEL), row),
            _layer_param((1, D_MODEL), g_layer),
            _layer_param((D_MODEL, D_IN), layer),
            _layer_param((1, N_BRANCH * D_MODEL), g_layer),
            *cast_in_specs,
        ],
        out_specs=out_specs + cast_out_specs,
        out_shape=outs,
        scratch_shapes=[pltpu.VMEM((3 * (_N_GROUPS - 1), tm, LANES), F32)],
        compiler_params=_params(),
        name="inproj",
    )(x2, g, w_in, b_gate, *(a for a, _ in casts))


def _pair_scores(q2, k2):
    lane = lax.broadcasted_iota(jnp.int32, q2.shape, 1)
    zero = jnp.zeros_like(q2)
    qs = jnp.concatenate([jnp.where(lane < HEAD_DIM, q2, zero),
                          jnp.where(lane < HEAD_DIM, zero, q2)], axis=0)
    return _dot_nt(qs, k2)


def _softmax_parts(s):
    m = jnp.max(s, axis=-1, keepdims=True)
    return m, jnp.exp2(s - m).astype(BF16)


def _pv_and_den(p, v2):
    r = _dot(p, jnp.concatenate([v2, jnp.ones_like(v2)], axis=1))
    return r[:, :LANES], r[:, LANES:]


def _unstack_heads(a):
    m_rows = a.shape[0] // 2
    lane = lax.broadcasted_iota(jnp.int32, (m_rows, LANES), 1)
    return jnp.where(lane < HEAD_DIM, a[:m_rows], a[m_rows:])


NA_ROWS_PER_STEP = 16
NA_GROUP = 16
NA_BAND = NA_KH * GRID_W


NA_RPB_PAD = GRID_W - NA_KW


def _na_build_bias(rp_ref, toe_ref, tab_ref):
    lane = lax.broadcasted_iota(jnp.int32, (GRID_W, LANES), 1)
    q = lax.broadcasted_iota(jnp.int32, (GRID_W, LANES), 0)
    c = lane % GRID_W
    c0 = jnp.clip(q - NA_KW // 2, 0, GRID_W - NA_KW)
    valid = (c >= c0) & (c < c0 + NA_KW)
    base_lo, base_hi = LANES - (GRID_W - 1), LANES - (GRID_W - 1) - GRID_W

    def toe_body(ro, carry):
        for h in range(NA_HEADS):
            row = jnp.broadcast_to(rp_ref[0, h, pl.ds(ro, 1), :], (GRID_W, LANES))
            lo = pltpu.roll(row, base_lo, 1, stride=1, stride_axis=0)
            hi = pltpu.roll(row, base_hi, 1, stride=1, stride_axis=0)
            toe_ref[h, ro] = jnp.where(valid, jnp.where(lane < GRID_W, lo, hi), NEG_INF)
        return carry

    lax.fori_loop(0, 2 * NA_KH - 1, toe_body, 0)

    def tab_body(dl, carry):
        for h in range(NA_HEADS):
            rows_ = slice((h % 2) * GRID_W, (h % 2 + 1) * GRID_W)
            for g in range(NA_KH // 2):
                even, odd = toe_ref[h, dl + 2 * g], toe_ref[h, dl + 2 * g + 1]
                tab_ref[h // 2, dl, rows_, g * LANES:(g + 1) * LANES] = \
                    jnp.where(lane < GRID_W, even, odd)
        return carry

    lax.fori_loop(0, NA_KH, tab_body, 0)


def _na_kernel(q_ref, k_ref, v_ref, rp_ref, o_ref, toe_ref, tab_ref, *, rows):
    rb = pl.program_id(1)

    @pl.when((pl.program_id(0) == 0) & (rb == 0))
    def _():
        _na_build_bias(rp_ref, toe_ref, tab_ref)

    def group_body(gi, carry):
        chains = []
        for lg in range(NA_GROUP):
            lr = gi * NA_GROUP + lg
            i = rb * NA_ROWS_PER_STEP + lr
            r0 = jnp.clip(i - NA_KH // 2, 0, rows - NA_KH)
            ks = pl.multiple_of(r0 * GRID_W, GRID_W)
            qs = pl.multiple_of(lr * GRID_W, GRID_W)
            for hp in range(NA_HEADS // 2):
                chains.append((qs, ks, r0 - i + (NA_KH - 1), hp,
                               slice(hp * LANES, (hp + 1) * LANES)))
        ss = [_pair_scores(q_ref[0, pl.ds(qs, GRID_W), cs], k_ref[0, pl.ds(ks, NA_BAND), cs])
              for qs, ks, dl, hp, cs in chains]
        parts = [_softmax_parts(s + tab_ref[hp, dl])
                 for s, (qs, ks, dl, hp, cs) in zip(ss, chains)]
        for (_, p), (qs, ks, dl, hp, cs) in zip(parts, chains):
            pv, den = _pv_and_den(p, v_ref[0, pl.ds(ks, NA_BAND), cs])
            o_ref[0, pl.ds(qs, GRID_W), cs] = _unstack_heads(pv / den).astype(BF16)
        return carry

    lax.fori_loop(0, NA_ROWS_PER_STEP // NA_GROUP, group_body, 0)


def _na_attention(q, k, v, rp, layer, b, s):
    rows = s // GRID_W
    qblk = NA_ROWS_PER_STEP * GRID_W
    q3, k3, v3 = (a.reshape(b, s, D_A) for a in (q, k, v))
    out = pl.pallas_call(
        functools.partial(_na_kernel, rows=rows),
        grid=(b, rows // NA_ROWS_PER_STEP),
        in_specs=[
            pl.BlockSpec((1, qblk, D_A), lambda i, j: (i, j, 0)),
            pl.BlockSpec((1, s, D_A), lambda i, j: (i, 0, 0)),
            pl.BlockSpec((1, s, D_A), lambda i, j: (i, 0, 0)),
            _layer_param(rp.shape[1:], layer),
        ],
        out_specs=pl.BlockSpec((1, qblk, D_A), lambda i, j: (i, j, 0)),
        out_shape=jax.ShapeDtypeStruct((b, s, D_A), BF16),
        scratch_shapes=[pltpu.VMEM((NA_HEADS, 2 * NA_KH - 1, GRID_W, LANES), F32),
                        pltpu.VMEM((NA_HEADS // 2, NA_KH, 2 * GRID_W, NA_BAND), F32)],
        compiler_params=_params(2),
        name="na_attn",
    )(q3, k3, v3, rp)
    return out.reshape(b * s, D_A)


DIL_QB = 128
DIL_HALF = 64
DIL_KB = DIL_QB + 2 * DIL_HALF
DIL_GROUP = 32
DIL_KEY_SHIFTS = (0, -1, -2)


def _alibi_slope(head):
    return 2.0 ** (-8.0 * (head + 1) / DIL_HEADS)


def _dil_kernel(x_ref, o_ref, lse_ref, bias_ref, *, length, dilation, slopes, residues):
    nblk = length // DIL_QB

    @pl.when((pl.program_id(0) == 0) & (pl.program_id(1) == 0))
    def _():
        rel0 = (lax.broadcasted_iota(jnp.int32, (DIL_QB, DIL_KB), 1)
                - lax.broadcasted_iota(jnp.int32, (DIL_QB, DIL_KB), 0))
        for c, shift in enumerate(DIL_KEY_SHIFTS):
            dist = jnp.abs(rel0 + shift * DIL_HALF)
            dist_f = (dist * dilation).astype(F32)
            for hh in range(2):
                bias = jnp.where(dist <= DIL_HALF, (-slopes[hh] * LOG2E) * dist_f, NEG_INF)
                bias_ref[c, hh * DIL_QB:(hh + 1) * DIL_QB, :] = bias

    def group(chains):
        geo = []
        for r, bi in chains:
            q0 = bi * DIL_QB
            if isinstance(bi, int):
                ks = min(max(q0 - DIL_HALF, 0), length - DIL_KB)
            else:
                q0 = pl.multiple_of(q0, DIL_QB)
                ks = pl.multiple_of(jnp.clip(q0 - DIL_HALF, 0, length - DIL_KB), DIL_HALF)
            geo.append((r, q0, ks, (q0 - ks) // DIL_HALF))
        ss = [_pair_scores(x_ref[0, r, pl.ds(q0, DIL_QB), 0:LANES],
                           x_ref[0, r, pl.ds(ks, DIL_KB), LANES:2 * LANES])
              for r, q0, ks, case in geo]
        parts = [_softmax_parts(s + bias_ref[case]) for s, (r, q0, ks, case) in zip(ss, geo)]
        for (m, p), (r, q0, ks, case) in zip(parts, geo):
            pv, den = _pv_and_den(p, x_ref[0, r, pl.ds(ks, DIL_KB), 2 * LANES:3 * LANES])
            o_ref[0, r, pl.ds(q0, DIL_QB), :] = _unstack_heads(pv / den).astype(BF16)
            lse_ref[0, r, pl.ds(q0, DIL_QB), :] = _unstack_heads(m * LN2 + jnp.log(den))

    if residues * nblk <= DIL_GROUP:
        group([(r, bi) for r in range(residues) for bi in range(nblk)])
    else:
        assert nblk % DIL_GROUP == 0
        for r in range(residues):
            def body(gi, carry, r=r):
                group([(r, gi * DIL_GROUP + j) for j in range(DIL_GROUP)])
                return carry
            lax.fori_loop(0, nblk // DIL_GROUP, body, 0)


def _dil_attention(qkv, grp):
    window, dilation = DIL_PAIRS[grp]
    assert window // 2 // dilation == DIL_HALF
    b, _, length, width = qkv.shape
    assert length % DIL_QB == 0 and length >= DIL_KB
    nblk = length // DIL_QB
    residues = min(dilation, max(1, DIL_GROUP // nblk))
    assert dilation % residues == 0
    slopes = tuple(_alibi_slope(2 * grp + hh) for hh in range(2))
    blk = lambda w: pl.BlockSpec((1, residues, length, w), lambda i, r: (i, r, 0, 0))
    return pl.pallas_call(
        functools.partial(_dil_kernel, length=length, dilation=dilation, slopes=slopes,
                          residues=residues),
        grid=(b, dilation // residues),
        in_specs=[blk(width)],
        out_specs=[blk(LANES), blk(LANES)],
        out_shape=[jax.ShapeDtypeStruct((b, dilation, length, LANES), BF16),
                   jax.ShapeDtypeStruct((b, dilation, length, LANES), F32)],
        scratch_shapes=[pltpu.VMEM((len(DIL_KEY_SHIFTS), 2 * DIL_QB, DIL_KB), F32)],
        compiler_params=_params(2),
        name=f"dil_attn_{grp}",
    )(qkv)


def _merge_kernel(x_ref, oa_ref, o0_ref, o1_ref, o2_ref, l0_ref, l1_ref, l2_ref, qm_ref, gate_ref,
                  kv_ref, wa_ref, wb_ref, wm_ref, wo_ref, g_ref, out_ref, stage_ref):
    blocks = [slice(r0, r0 + MG_SUB) for r0 in range(0, MG_TM, MG_SUB)]
    pairs = range(MEM_HEADS // 2)

    def natural_order(ref, grp, slot):
        d = DIL_PAIRS[grp][1]
        if d == 1:
            return ref[0, 0].astype(F32)
        stage = stage_ref.at[slot]
        for r in range(d):
            stage[pl.ds(r, MG_TM // d, stride=d), :] = ref[0, r].astype(F32)
        return stage[...]

    o_refs, l_refs = (o0_ref, o1_ref, o2_ref), (l0_ref, l1_ref, l2_ref)
    os_ = [natural_order(o_refs[g], g, 2 * (g - 1)) for g in range(_N_GROUPS)]
    ls = [natural_order(l_refs[g], g, 2 * (g - 1) + 1) for g in range(_N_GROUPS)]
    o_b = []
    for rows in blocks:
        l0, l1, l2 = (l[rows] for l in ls)
        mx = jnp.maximum(jnp.maximum(l0, l1), l2)
        e0, e1, e2 = jnp.exp(l0 - mx), jnp.exp(l1 - mx), jnp.exp(l2 - mx)
        inv = 1.0 / (e0 + e1 + e2)
        o_b.append(((e0 * inv) * os_[0][rows] + (e1 * inv) * os_[1][rows]
                    + (e2 * inv) * os_[2][rows]).astype(BF16))

    proj_b = [_dot(o, wb_ref[0]) for o in o_b]
    ss = [[_pair_scores(qm_ref[rows, hp * LANES:(hp + 1) * LANES],
                        kv_ref[0, 0, :, hp * LANES:(hp + 1) * LANES]) for hp in pairs]
          for rows in blocks]
    proj_a = [_dot(oa_ref[rows, :], wa_ref[0]) for rows in blocks]

    o_m = []
    for blk_ss in ss:
        om = []
        for hp, s in zip(pairs, blk_ss):
            _, p = _softmax_parts(s)
            pv, den = _pv_and_den(p, kv_ref[0, 0, :, D_M + hp * LANES:D_M + (hp + 1) * LANES])
            om.append(_unstack_heads(pv / den).astype(BF16))
        o_m.append(jnp.concatenate(om, axis=-1))
    proj_m = [_dot(o, wm_ref[0]) for o in o_m]

    merged = [(gate_ref[rows, 0:D_MODEL].astype(F32) * pa
               + gate_ref[rows, D_MODEL:2 * D_MODEL].astype(F32) * pb
               + gate_ref[rows, 2 * D_MODEL:3 * D_MODEL].astype(F32) * pm).astype(BF16)
              for rows, pa, pb, pm in zip(blocks, proj_a, proj_b, proj_m)]
    for rows, mg in zip(blocks, merged):
        y = _dot(mg, wo_ref[0])
        out_ref[rows, :] = x_ref[rows, :] + _rms(y, g_ref[0])


def _merge(x2, oa, ob, lse, qm, gates, kv, wa, wb, wm, wo, g, layer, s):
    t = x2.shape[0]
    tiles_per_seq = s // MG_TM
    row = lambda i: (i, 0)
    dil = [pl.BlockSpec((1, d, MG_TM // d, LANES),
                        lambda i: (i // tiles_per_seq, 0, i % tiles_per_seq, 0))
           for _, d in DIL_PAIRS]
    in_specs = [
        pl.BlockSpec((MG_TM, D_MODEL), row),
        pl.BlockSpec((MG_TM, D_A), row),
        *dil, *dil,
        pl.BlockSpec((MG_TM, D_M), row),
        pl.BlockSpec((MG_TM, N_BRANCH * D_MODEL), row),
        pl.BlockSpec((1, 1, N_MEM, 2 * D_M), lambda i: (layer, i // tiles_per_seq, 0, 0)),
        _layer_param((D_A, D_MODEL), layer), _layer_param((D_B_OUT, D_MODEL), layer),
        _layer_param((D_M, D_MODEL), layer), _layer_param((D_MODEL, D_MODEL), layer),
        _layer_param((1, D_MODEL), layer),
    ]
    return pl.pallas_call(
        _merge_kernel,
        grid=(t // MG_TM,),
        in_specs=in_specs,
        out_specs=pl.BlockSpec((MG_TM, D_MODEL), row),
        out_shape=jax.ShapeDtypeStruct((t, D_MODEL), F32),
        scratch_shapes=[pltpu.VMEM((2 * (_N_GROUPS - 1), MG_TM, LANES), F32)],
        compiler_params=_params(),
        name="merge",
    )(x2, oa, *ob, *lse, qm, gates, kv, wa, wb, wm, wo, g)


FF_CHUNK = 256
GELU_C0 = 0.7978845608028654
GELU_C1 = GELU_C0 * 0.044715


def _gelu_gate(a, half_b):
    t = jnp.tanh(a * (GELU_C0 + GELU_C1 * (a * a)))
    hb = a * half_b
    return hb + hb * t


def _ffn_kernel(xm_ref, xp_ref, xn_ref, g1_ref, wup_ref, cw_ref, cb_ref, wdn_ref, g2_ref, out_ref,
                hext_ref, f_ref, *, tiles_per_seq):
    i = pl.program_id(0)
    pos = i % tiles_per_seq
    g1 = g1_ref[0]
    x = xm_ref[...]
    hext_ref[0:FF_TM, :] = _rms(x, g1).astype(BF16)
    hn = jnp.where(pos == tiles_per_seq - 1, 0.0, _rms(xn_ref[...], g1))
    hp = jnp.where(pos == 0, 0.0, _rms(xp_ref[...], g1))
    hext_ref[FF_TM:FF_TM + 2 * SUB, :] = jnp.concatenate([hn, hp], axis=0).astype(BF16)
    hext = hext_ref[...]
    ext = FF_TM + 2 * SUB

    def conv(u, col):
        cs = slice(col, col + FF_CHUNK)
        before = pltpu.roll(u, 1, 0)[0:FF_TM]
        after = pltpu.roll(u, ext - 1, 0)[0:FF_TM]
        y = cb_ref[0, :, cs] + before * cw_ref[0, 0:1, cs]
        y = y + u[0:FF_TM] * cw_ref[0, 1:2, cs]
        return y + after * cw_ref[0, 2:3, cs]

    for c in range(D_FF // FF_CHUNK):
        ca, cb = c * FF_CHUNK, D_FF + c * FF_CHUNK
        ua = conv(_dot(hext, wup_ref[0, :, ca:ca + FF_CHUNK]), ca)
        ub = conv(_dot(hext, wup_ref[0, :, cb:cb + FF_CHUNK]), cb)
        f_ref[:, ca:ca + FF_CHUNK] = _gelu_gate(ua, ub).astype(BF16)

    y = _dot(f_ref[...], wdn_ref[0])
    out_ref[...] = x + _rms(y, g2_ref[0])


def _ffn(x2, g1, w_up, cw, cb, w_dn, g2, layer, s):
    t = x2.shape[0]
    assert s % FF_TM == 0
    tiles_per_seq = s // FF_TM
    per = FF_TM // SUB
    nh = t // SUB
    row = lambda i: (i, 0)
    return pl.pallas_call(
        functools.partial(_ffn_kernel, tiles_per_seq=tiles_per_seq),
        grid=(t // FF_TM,),
        in_specs=[
            pl.BlockSpec((FF_TM, D_MODEL), row),
            pl.BlockSpec((SUB, D_MODEL), lambda i: (jnp.maximum(i * per - 1, 0), 0)),
            pl.BlockSpec((SUB, D_MODEL), lambda i: (jnp.minimum((i + 1) * per, nh - 1), 0)),
            _layer_param((1, D_MODEL), layer),
            _layer_param((D_MODEL, 2 * D_FF), layer),
            _layer_param((3, 2 * D_FF), layer),
            _layer_param((1, 2 * D_FF), layer),
            _layer_param((D_FF, D_MODEL), layer),
            _layer_param((1, D_MODEL), layer),
        ],
        out_specs=pl.BlockSpec((FF_TM, D_MODEL), row),
        out_shape=jax.ShapeDtypeStruct((t, D_MODEL), F32),
        scratch_shapes=[pltpu.VMEM((FF_TM + 2 * SUB, D_MODEL), BF16),
                        pltpu.VMEM((FF_TM, D_FF), BF16)],
        compiler_params=_params(),
        name="ffn",
    )(x2, x2, x2, g1, w_up, cw, cb, w_dn, g2)


def kernel(x, mem, mem_norm_g, g_pre_mix, w_in, rpb_na, w_mem_kv, b_gate, w_br_a, w_br_b, w_br_m,
           w_out, g_post_mix, g_pre_ffn, w_up, conv_w, conv_b, w_down, g_post_ffn):
    b, s, d = x.shape
    depth = w_in.shape[0]
    assert d == D_MODEL and s % GRID_W == 0 and s % MG_TM == 0
    rows = s // GRID_W
    assert rows >= NA_KH and rows % NA_ROWS_PER_STEP == 0

    bf = lambda a: a.astype(BF16)
    vec = lambda a: a.reshape(depth, 1, -1).astype(F32)

    w_in_first = bf(w_in[:1])
    late_casts = [(w_up.reshape(depth * D_MODEL, 2 * D_FF), 0),
                  (w_down.reshape(depth * D_FF, D_MODEL), 0),
                  (w_out.reshape(depth * D_MODEL, D_MODEL), 0)]
    if depth > 1:
        late_casts.append((w_in.reshape(depth * D_MODEL, D_IN), D_MODEL))
    w_br_a, w_br_b, w_br_m = bf(w_br_a), bf(w_br_b), bf(w_br_m)
    g_pre_mix, g_post_mix, g_pre_ffn, g_post_ffn = map(vec, (g_pre_mix, g_post_mix, g_pre_ffn,
                                                             g_post_ffn))
    half = jnp.concatenate([jnp.ones((D_FF,), F32), jnp.full((D_FF,), 0.5, F32)])
    b_gate, conv_b, conv_w = vec(b_gate), vec(conv_b * half), conv_w.astype(F32) * half
    assert rpb_na.shape[1:] == (NA_HEADS, 2 * NA_KH - 1, 2 * NA_KW - 1)
    rp = jnp.pad(rpb_na.astype(F32) * LOG2E, ((0, 0), (0, 0), (0, 0),
                                      (NA_RPB_PAD, LANES - NA_RPB_PAD - (2 * NA_KW - 1))))
    kv = _memkv(mem, mem_norm_g.reshape(1, -1).astype(F32), bf(w_mem_kv))

    x2 = x.reshape(b * s, d)
    for l in range(depth):
        if l == 0:
            *proj, w_up, w_down, w_out, w_in_rest = (*_inproj(
                x2, g_pre_mix, w_in_first, b_gate, 0, 0, b, s, IN_TM_FIRST, late_casts),
                *([None] * (depth == 1)))
            w_up = w_up.reshape(depth, D_MODEL, 2 * D_FF)
            w_down = w_down.reshape(depth, D_FF, D_MODEL)
            w_out = w_out.reshape(depth, D_MODEL, D_MODEL)
            if depth > 1:
                w_in_rest = w_in_rest.reshape(depth - 1, D_MODEL, D_IN)
        else:
            proj = _inproj(x2, g_pre_mix, w_in_rest, b_gate, l - 1, l, b, s, IN_TM)
        qa, ka, va, b0, b1, b2, qm, gates = proj
        oa = _na_attention(qa, ka, va, rp, l, b, s)
        ob, lse = zip(*(_dil_attention(q, grp) for grp, q in enumerate((b0, b1, b2))))
        x2 = _merge(x2, oa, ob, lse, qm, gates, kv, w_br_a, w_br_b, w_br_m, w_out, g_post_mix, l, s)
        x2 = _ffn(x2, g_pre_ffn, w_up, conv_w, conv_b, w_down, g_post_ffn, l, s)
    return x2.reshape(b, s, d)
```

```python
import functools

import jax
import jax.numpy as jnp
from jax import lax
from jax.experimental import pallas as pl
from jax.experimental.pallas import tpu as pltpu

D_MODEL = 1024
GRID_W = 64
N_MEM = 256
HEAD_DIM = 64
NA_HEADS = 6
NA_KH = 8
NA_KW = 16
DIL_PAIRS = ((128, 1), (512, 4), (2048, 16))
DIL_HEADS = 6
MEM_HEADS = 4
D_A = 384
D_B = 384
D_B_OUT = 128
D_M = 256
N_BRANCH = 3
D_IN = 3 * D_A + 3 * D_B + D_M + N_BRANCH * D_MODEL
D_FF = 2816
RMS_EPS = 1e-6
NEG_INF = -1e30

LANES = 128
SUB = 8
SUB_BF16 = 16
VMEM_LIMIT = 56 * 1024 * 1024
IN_TM = 1024
IN_TM_FIRST = 512
MG_TM = 1024
MG_SUB = 512
FF_TM = 1024

F32 = jnp.float32
BF16 = jnp.bfloat16

LOG2E = 1.4426950408889634
LN2 = 0.6931471805599453
Q_SCALE = HEAD_DIM ** -0.5 * LOG2E


def _rms(x, g):
    return x * lax.rsqrt(jnp.mean(x * x, axis=-1, keepdims=True) + RMS_EPS) * g


def _dot(a, b):
    return jnp.dot(a, b, preferred_element_type=F32)


def _dot_nt(a, b):
    return lax.dot_general(a, b, (((1,), (1,)), ((), ())), preferred_element_type=F32)


def _layer_param(shape, layer):
    zeros = (0,) * len(shape)
    return pl.BlockSpec((1,) + tuple(shape), lambda *_: (layer,) + zeros,
                        pipeline_mode=pl.Buffered(1))


def _params(n_axes=1):
    return pltpu.CompilerParams(dimension_semantics=("arbitrary",) * n_axes,
                                vmem_limit_bytes=VMEM_LIMIT)


def _memkv_kernel(mem_ref, g_ref, w_ref, o_ref):
    mn = _rms(mem_ref[0], g_ref[...]).astype(BF16)
    o_ref[0, 0] = _dot(mn, w_ref[0]).astype(BF16)


def _memkv(mem, g, w_kv):
    depth = w_kv.shape[0]
    b = mem.shape[0]
    return pl.pallas_call(
        _memkv_kernel,
        grid=(depth, b),
        in_specs=[
            pl.BlockSpec((1, N_MEM, D_MODEL), lambda l, i: (i, 0, 0)),
            pl.BlockSpec((1, D_MODEL), lambda l, i: (0, 0)),
            pl.BlockSpec((1, D_MODEL, 2 * D_M), lambda l, i: (l, 0, 0)),
        ],
        out_specs=pl.BlockSpec((1, 1, N_MEM, 2 * D_M), lambda l, i: (l, i, 0, 0)),
        out_shape=jax.ShapeDtypeStruct((depth, b, N_MEM, 2 * D_M), BF16),
        compiler_params=_params(2),
        name="memkv",
    )(mem, g, w_kv)


IN_CHUNK = 512
_A_SLABS = D_A // LANES
_M_SLABS = D_M // LANES
_N_GROUPS = len(DIL_PAIRS)


def _inproj_kernel(*refs, n_casts):
    x_ref, g_ref, w_ref, bg_ref = refs[:4]
    cast_in = refs[4:4 + n_casts]
    qa_ref, ka_ref, va_ref, b0_ref, b1_ref, b2_ref, qm_ref, gate_ref = refs[4 + n_casts:12 + n_casts]
    cast_out = refs[12 + n_casts:12 + 2 * n_casts]
    stage_ref = refs[-1]
    tm = x_ref.shape[0]
    x = x_ref[...]
    h = (x * g_ref[0]).astype(BF16)
    inv = jnp.broadcast_to(lax.rsqrt(jnp.mean(x * x, axis=-1, keepdims=True) + RMS_EPS),
                           (tm, LANES))
    inv_q = inv * Q_SCALE
    b_refs = (b0_ref, b1_ref, b2_ref)

    def put_dilated(which, grp, val):
        d = DIL_PAIRS[grp][1]
        cs = slice(which * LANES, (which + 1) * LANES)
        if d == 1:
            b_refs[grp][0, 0, :, cs] = val.astype(BF16)
            return
        stage = stage_ref.at[(grp - 1) * 3 + which]
        stage[...] = val
        for r in range(d):
            b_refs[grp][0, r, :, cs] = stage[pl.ds(r, tm // d, stride=d), :].astype(BF16)

    def put(slab, val):
        s = slab
        if s < 3 * _A_SLABS:
            which, j = divmod(s, _A_SLABS)
            ref = (qa_ref, ka_ref, va_ref)[which]
            ref[:, j * LANES:(j + 1) * LANES] = (val * (inv_q if which == 0 else inv)).astype(BF16)
            return
        s -= 3 * _A_SLABS
        if s < 3 * _N_GROUPS:
            which, grp = divmod(s, _N_GROUPS)
            put_dilated(which, grp, val * (inv_q if which == 0 else inv))
            return
        s -= 3 * _N_GROUPS
        if s < _M_SLABS:
            qm_ref[:, s * LANES:(s + 1) * LANES] = (val * inv_q).astype(BF16)
            return
        s -= _M_SLABS
        gate = jax.nn.sigmoid(val * inv + bg_ref[0, :, s * LANES:(s + 1) * LANES])
        gate_ref[:, s * LANES:(s + 1) * LANES] = gate.astype(BF16)

    per = IN_CHUNK // LANES
    for c in reversed(range(D_IN // IN_CHUNK)):
        r = _dot(h, w_ref[0, :, c * IN_CHUNK:(c + 1) * IN_CHUNK])
        for j in range(per):
            put(c * per + j, r[:, j * LANES:(j + 1) * LANES])

    for src, dst in zip(cast_in, cast_out):
        dst[...] = src[...].astype(BF16)


def _inproj(x2, g, w_in, b_gate, layer, g_layer, b, s, tm, casts=()):
    t = x2.shape[0]
    assert s % tm == 0
    tiles_per_seq = s // tm
    steps = t // tm
    row = lambda i: (i, 0)
    outs = [jax.ShapeDtypeStruct((t, D_A), BF16)] * 3
    out_specs = [pl.BlockSpec((tm, D_A), row)] * 3
    for _, d in DIL_PAIRS:
        assert tm % (d * SUB_BF16) == 0
        outs.append(jax.ShapeDtypeStruct((b, d, s // d, 3 * LANES), BF16))
        out_specs.append(pl.BlockSpec((1, d, tm // d, 3 * LANES),
                                      lambda i: (i // tiles_per_seq, 0, i % tiles_per_seq, 0)))
    outs += [jax.ShapeDtypeStruct((t, D_M), BF16),
             jax.ShapeDtypeStruct((t, N_BRANCH * D_MODEL), BF16)]
    out_specs += [pl.BlockSpec((tm, D_M), row),
                  pl.BlockSpec((tm, N_BRANCH * D_MODEL), row)]
    cast_in_specs, cast_out_specs = [], []
    for a, skip in casts:
        rows, cols = a.shape
        blk = (rows - skip) // steps
        assert (rows - skip) % (steps * SUB_BF16) == 0 and skip % blk == 0
        cast_in_specs.append(pl.BlockSpec((blk, cols), lambda i, o=skip // blk: (o + i, 0)))
        cast_out_specs.append(pl.BlockSpec((blk, cols), row))
        outs.append(jax.ShapeDtypeStruct((rows - skip, cols), BF16))
    return pl.pallas_call(
        functools.partial(_inproj_kernel, n_casts=len(casts)),
        grid=(steps,),
        in_specs=[
            pl.BlockSpec((tm, D_MODEL), row),
            _layer_param((1, D_MODEL), g_layer),
            _layer_param((D_MODEL, D_IN), layer),
            _layer_param((1, N_BRANCH * D_MODEL), g_layer),
            *cast_in_specs,
        ],
        out_specs=out_specs + cast_out_specs,
        out_shape=outs,
        scratch_shapes=[pltpu.VMEM((3 * (_N_GROUPS - 1), tm, LANES), F32)],
        compiler_params=_params(),
        name="inproj",
    )(x2, g, w_in, b_gate, *(a for a, _ in casts))


def _pair_scores(q2, k2):
    lane = lax.broadcasted_iota(jnp.int32, q2.shape, 1)
    zero = jnp.zeros_like(q2)
    qs = jnp.concatenate([jnp.where(lane < HEAD_DIM, q2, zero),
                          jnp.where(lane < HEAD_DIM, zero, q2)], axis=0)
    return _dot_nt(qs, k2)


def _softmax_parts(s):
    m = jnp.max(s, axis=-1, keepdims=True)
    return m, jnp.exp2(s - m).astype(BF16)


def _pv_and_den(p, v2):
    r = _dot(p, jnp.concatenate([v2, jnp.ones_like(v2)], axis=1))
    return r[:, :LANES], r[:, LANES:]


def _unstack_heads(a):
    m_rows = a.shape[0] // 2
    lane = lax.broadcasted_iota(jnp.int32, (m_rows, LANES), 1)
    return jnp.where(lane < HEAD_DIM, a[:m_rows], a[m_rows:])


NA_ROWS_PER_STEP = 16
NA_GROUP = 16
NA_BAND = NA_KH * GRID_W


NA_RPB_PAD = GRID_W - NA_KW


def _na_build_bias(rp_ref, toe_ref, tab_ref):
    lane = lax.broadcasted_iota(jnp.int32, (GRID_W, LANES), 1)
    q = lax.broadcasted_iota(jnp.int32, (GRID_W, LANES), 0)
    c = lane % GRID_W
    c0 = jnp.clip(q - NA_KW // 2, 0, GRID_W - NA_KW)
    valid = (c >= c0) & (c < c0 + NA_KW)
    base_lo, base_hi = LANES - (GRID_W - 1), LANES - (GRID_W - 1) - GRID_W

    def toe_body(ro, carry):
        for h in range(NA_HEADS):
            row = jnp.broadcast_to(rp_ref[0, h, pl.ds(ro, 1), :], (GRID_W, LANES))
            lo = pltpu.roll(row, base_lo, 1, stride=1, stride_axis=0)
            hi = pltpu.roll(row, base_hi, 1, stride=1, stride_axis=0)
            toe_ref[h, ro] = jnp.where(valid, jnp.where(lane < GRID_W, lo, hi), NEG_INF)
        return carry

    lax.fori_loop(0, 2 * NA_KH - 1, toe_body, 0)

    def tab_body(dl, carry):
        for h in range(NA_HEADS):
            rows_ = slice((h % 2) * GRID_W, (h % 2 + 1) * GRID_W)
            for g in range(NA_KH // 2):
                even, odd = toe_ref[h, dl + 2 * g], toe_ref[h, dl + 2 * g + 1]
                tab_ref[h // 2, dl, rows_, g * LANES:(g + 1) * LANES] = \
                    jnp.where(lane < GRID_W, even, odd)
        return carry

    lax.fori_loop(0, NA_KH, tab_body, 0)


def _na_kernel(q_ref, k_ref, v_ref, rp_ref, o_ref, toe_ref, tab_ref, *, rows):
    rb = pl.program_id(1)

    @pl.when((pl.program_id(0) == 0) & (rb == 0))
    def _():
        _na_build_bias(rp_ref, toe_ref, tab_ref)

    def group_body(gi, carry):
        chains = []
        for lg in range(NA_GROUP):
            lr = gi * NA_GROUP + lg
            i = rb * NA_ROWS_PER_STEP + lr
            r0 = jnp.clip(i - NA_KH // 2, 0, rows - NA_KH)
            ks = pl.multiple_of(r0 * GRID_W, GRID_W)
            qs = pl.multiple_of(lr * GRID_W, GRID_W)
            for hp in range(NA_HEADS // 2):
                chains.append((qs, ks, r0 - i + (NA_KH - 1), hp,
                               slice(hp * LANES, (hp + 1) * LANES)))
        ss = [_pair_scores(q_ref[0, pl.ds(qs, GRID_W), cs], k_ref[0, pl.ds(ks, NA_BAND), cs])
              for qs, ks, dl, hp, cs in chains]
        parts = [_softmax_parts(s + tab_ref[hp, dl])
                 for s, (qs, ks, dl, hp, cs) in zip(ss, chains)]
        for (_, p), (qs, ks, dl, hp, cs) in zip(parts, chains):
            pv, den = _pv_and_den(p, v_ref[0, pl.ds(ks, NA_BAND), cs])
            o_ref[0, pl.ds(qs, GRID_W), cs] = _unstack_heads(pv / den).astype(BF16)
        return carry

    lax.fori_loop(0, NA_ROWS_PER_STEP // NA_GROUP, group_body, 0)


def _na_attention(q, k, v, rp, layer, b, s):
    rows = s // GRID_W
    qblk = NA_ROWS_PER_STEP * GRID_W
    q3, k3, v3 = (a.reshape(b, s, D_A) for a in (q, k, v))
    out = pl.pallas_call(
        functools.partial(_na_kernel, rows=rows),
        grid=(b, rows // NA_ROWS_PER_STEP),
        in_specs=[
            pl.BlockSpec((1, qblk, D_A), lambda i, j: (i, j, 0)),
            pl.BlockSpec((1, s, D_A), lambda i, j: (i, 0, 0)),
            pl.BlockSpec((1, s, D_A), lambda i, j: (i, 0, 0)),
            _layer_param(rp.shape[1:], layer),
        ],
        out_specs=pl.BlockSpec((1, qblk, D_A), lambda i, j: (i, j, 0)),
        out_shape=jax.ShapeDtypeStruct((b, s, D_A), BF16),
        scratch_shapes=[pltpu.VMEM((NA_HEADS, 2 * NA_KH - 1, GRID_W, LANES), F32),
                        pltpu.VMEM((NA_HEADS // 2, NA_KH, 2 * GRID_W, NA_BAND), F32)],
        compiler_params=_params(2),
        name="na_attn",
    )(q3, k3, v3, rp)
    return out.reshape(b * s, D_A)


DIL_QB = 128
DIL_HALF = 64
DIL_KB = DIL_QB + 2 * DIL_HALF
DIL_GROUP = 32
DIL_KEY_SHIFTS = (0, -1, -2)


def _alibi_slope(head):
    return 2.0 ** (-8.0 * (head + 1) / DIL_HEADS)


def _dil_kernel(x_ref, o_ref, lse_ref, bias_ref, *, length, dilation, slopes, residues):
    nblk = length // DIL_QB

    @pl.when((pl.program_id(0) == 0) & (pl.program_id(1) == 0))
    def _():
        rel0 = (lax.broadcasted_iota(jnp.int32, (DIL_QB, DIL_KB), 1)
                - lax.broadcasted_iota(jnp.int32, (DIL_QB, DIL_KB), 0))
        for c, shift in enumerate(DIL_KEY_SHIFTS):
            dist = jnp.abs(rel0 + shift * DIL_HALF)
            dist_f = (dist * dilation).astype(F32)
            for hh in range(2):
                bias = jnp.where(dist <= DIL_HALF, (-slopes[hh] * LOG2E) * dist_f, NEG_INF)
                bias_ref[c, hh * DIL_QB:(hh + 1) * DIL_QB, :] = bias

    def group(chains):
        geo = []
        for r, bi in chains:
            q0 = bi * DIL_QB
            if isinstance(bi, int):
                ks = min(max(q0 - DIL_HALF, 0), length - DIL_KB)
            else:
                q0 = pl.multiple_of(q0, DIL_QB)
                ks = pl.multiple_of(jnp.clip(q0 - DIL_HALF, 0, length - DIL_KB), DIL_HALF)
            geo.append((r, q0, ks, (q0 - ks) // DIL_HALF))
        ss = [_pair_scores(x_ref[0, r, pl.ds(q0, DIL_QB), 0:LANES],
                           x_ref[0, r, pl.ds(ks, DIL_KB), LANES:2 * LANES])
              for r, q0, ks, case in geo]
        parts = [_softmax_parts(s + bias_ref[case]) for s, (r, q0, ks, case) in zip(ss, geo)]
        for (m, p), (r, q0, ks, case) in zip(parts, geo):
            pv, den = _pv_and_den(p, x_ref[0, r, pl.ds(ks, DIL_KB), 2 * LANES:3 * LANES])
            o_ref[0, r, pl.ds(q0, DIL_QB), :] = _unstack_heads(pv / den).astype(BF16)
            lse_ref[0, r, pl.ds(q0, DIL_QB), :] = _unstack_heads(m * LN2 + jnp.log(den))

    if residues * nblk <= DIL_GROUP:
        group([(r, bi) for r in range(residues) for bi in range(nblk)])
    else:
        assert nblk % DIL_GROUP == 0
        for r in range(residues):
            def body(gi, carry, r=r):
                group([(r, gi * DIL_GROUP + j) for j in range(DIL_GROUP)])
                return carry
            lax.fori_loop(0, nblk // DIL_GROUP, body, 0)


def _dil_attention(qkv, grp):
    window, dilation = DIL_PAIRS[grp]
    assert window // 2 // dilation == DIL_HALF
    b, _, length, width = qkv.shape
    assert length % DIL_QB == 0 and length >= DIL_KB
    nblk = length // DIL_QB
    residues = min(dilation, max(1, DIL_GROUP // nblk))
    assert dilation % residues == 0
    slopes = tuple(_alibi_slope(2 * grp + hh) for hh in range(2))
    blk = lambda w: pl.BlockSpec((1, residues, length, w), lambda i, r: (i, r, 0, 0))
    return pl.pallas_call(
        functools.partial(_dil_kernel, length=length, dilation=dilation, slopes=slopes,
                          residues=residues),
        grid=(b, dilation // residues),
        in_specs=[blk(width)],
        out_specs=[blk(LANES), blk(LANES)],
        out_shape=[jax.ShapeDtypeStruct((b, dilation, length, LANES), BF16),
                   jax.ShapeDtypeStruct((b, dilation, length, LANES), F32)],
        scratch_shapes=[pltpu.VMEM((len(DIL_KEY_SHIFTS), 2 * DIL_QB, DIL_KB), F32)],
        compiler_params=_params(2),
        name=f"dil_attn_{grp}",
    )(qkv)


def _merge_kernel(x_ref, oa_ref, o0_ref, o1_ref, o2_ref, l0_ref, l1_ref, l2_ref, qm_ref, gate_ref,
                  kv_ref, wa_ref, wb_ref, wm_ref, wo_ref, g_ref, out_ref, stage_ref):
    blocks = [slice(r0, r0 + MG_SUB) for r0 in range(0, MG_TM, MG_SUB)]
    pairs = range(MEM_HEADS // 2)

    def natural_order(ref, grp, slot):
        d = DIL_PAIRS[grp][1]
        if d == 1:
            return ref[0, 0].astype(F32)
        stage = stage_ref.at[slot]
        for r in range(d):
            stage[pl.ds(r, MG_TM // d, stride=d), :] = ref[0, r].astype(F32)
        return stage[...]

    o_refs, l_refs = (o0_ref, o1_ref, o2_ref), (l0_ref, l1_ref, l2_ref)
    os_ = [natural_order(o_refs[g], g, 2 * (g - 1)) for g in range(_N_GROUPS)]
    ls = [natural_order(l_refs[g], g, 2 * (g - 1) + 1) for g in range(_N_GROUPS)]
    o_b = []
    for rows in blocks:
        l0, l1, l2 = (l[rows] for l in ls)
        mx = jnp.maximum(jnp.maximum(l0, l1), l2)
        e0, e1, e2 = jnp.exp(l0 - mx), jnp.exp(l1 - mx), jnp.exp(l2 - mx)
        inv = 1.0 / (e0 + e1 + e2)
        o_b.append(((e0 * inv) * os_[0][rows] + (e1 * inv) * os_[1][rows]
                    + (e2 * inv) * os_[2][rows]).astype(BF16))

    proj_b = [_dot(o, wb_ref[0]) for o in o_b]
    ss = [[_pair_scores(qm_ref[rows, hp * LANES:(hp + 1) * LANES],
                        kv_ref[0, 0, :, hp * LANES:(hp + 1) * LANES]) for hp in pairs]
          for rows in blocks]
    proj_a = [_dot(oa_ref[rows, :], wa_ref[0]) for rows in blocks]

    o_m = []
    for blk_ss in ss:
        om = []
        for hp, s in zip(pairs, blk_ss):
            _, p = _softmax_parts(s)
            pv, den = _pv_and_den(p, kv_ref[0, 0, :, D_M + hp * LANES:D_M + (hp + 1) * LANES])
            om.append(_unstack_heads(pv / den).astype(BF16))
        o_m.append(jnp.concatenate(om, axis=-1))
    proj_m = [_dot(o, wm_ref[0]) for o in o_m]

    merged = [(gate_ref[rows, 0:D_MODEL].astype(F32) * pa
               + gate_ref[rows, D_MODEL:2 * D_MODEL].astype(F32) * pb
               + gate_ref[rows, 2 * D_MODEL:3 * D_MODEL].astype(F32) * pm).astype(BF16)
              for rows, pa, pb, pm in zip(blocks, proj_a, proj_b, proj_m)]
    for rows, mg in zip(blocks, merged):
        y = _dot(mg, wo_ref[0])
        out_ref[rows, :] = x_ref[rows, :] + _rms(y, g_ref[0])


def _merge(x2, oa, ob, lse, qm, gates, kv, wa, wb, wm, wo, g, layer, s):
    t = x2.shape[0]
    tiles_per_seq = s // MG_TM
    row = lambda i: (i, 0)
    dil = [pl.BlockSpec((1, d, MG_TM // d, LANES),
                        lambda i: (i // tiles_per_seq, 0, i % tiles_per_seq, 0))
           for _, d in DIL_PAIRS]
    in_specs = [
        pl.BlockSpec((MG_TM, D_MODEL), row),
        pl.BlockSpec((MG_TM, D_A), row),
        *dil, *dil,
        pl.BlockSpec((MG_TM, D_M), row),
        pl.BlockSpec((MG_TM, N_BRANCH * D_MODEL), row),
        pl.BlockSpec((1, 1, N_MEM, 2 * D_M), lambda i: (layer, i // tiles_per_seq, 0, 0)),
        _layer_param((D_A, D_MODEL), layer), _layer_param((D_B_OUT, D_MODEL), layer),
        _layer_param((D_M, D_MODEL), layer), _layer_param((D_MODEL, D_MODEL), layer),
        _layer_param((1, D_MODEL), layer),
    ]
    return pl.pallas_call(
        _merge_kernel,
        grid=(t // MG_TM,),
        in_specs=in_specs,
        out_specs=pl.BlockSpec((MG_TM, D_MODEL), row),
        out_shape=jax.ShapeDtypeStruct((t, D_MODEL), F32),
        scratch_shapes=[pltpu.VMEM((2 * (_N_GROUPS - 1), MG_TM, LANES), F32)],
        compiler_params=_params(),
        name="merge",
    )(x2, oa, *ob, *lse, qm, gates, kv, wa, wb, wm, wo, g)


FF_CHUNK = 256
FF_FIRST_ROWS = 256
GELU_C0 = 0.7978845608028654
GELU_C1 = GELU_C0 * 0.044715


def _gelu_gate(a, half_b):
    t = jnp.tanh(a * (GELU_C0 + GELU_C1 * (a * a)))
    hb = a * half_b
    return hb + hb * t


def _ffn_kernel(xm_ref, xp_ref, xn_ref, g1_ref, wup_ref, cw_ref, cb_ref, wdn_ref, g2_ref, out_ref,
                hext_ref, f_ref, *, tiles_per_seq):
    i = pl.program_id(0)
    pos = i % tiles_per_seq
    g1 = g1_ref[0]
    x = xm_ref[...]
    hext_ref[0:FF_TM, :] = _rms(x, g1).astype(BF16)
    hn = jnp.where(pos == tiles_per_seq - 1, 0.0, _rms(xn_ref[...], g1))
    hp = jnp.where(pos == 0, 0.0, _rms(xp_ref[...], g1))
    hext_ref[FF_TM:FF_TM + 2 * SUB, :] = jnp.concatenate([hn, hp], axis=0).astype(BF16)
    hext = hext_ref[...]
    ext = FF_TM + 2 * SUB

    def conv(u, col):
        cs = slice(col, col + FF_CHUNK)
        before = pltpu.roll(u, 1, 0)[0:FF_TM]
        after = pltpu.roll(u, ext - 1, 0)[0:FF_TM]
        y = cb_ref[0, :, cs] + before * cw_ref[0, 0:1, cs]
        y = y + u[0:FF_TM] * cw_ref[0, 1:2, cs]
        return y + after * cw_ref[0, 2:3, cs]

    def up(col, first):
        w = wup_ref[0, :, col:col + FF_CHUNK]
        if not first:
            return _dot(hext, w)
        edges = list(range(0, FF_TM, FF_FIRST_ROWS)) + [ext]
        return jnp.concatenate([_dot(hext_ref[a:b, :], w) for a, b in zip(edges, edges[1:])],
                               axis=0)

    for c in range(D_FF // FF_CHUNK):
        ca, cb = c * FF_CHUNK, D_FF + c * FF_CHUNK
        ua = conv(up(ca, c == 0), ca)
        ub = conv(up(cb, False), cb)
        f_ref[:, ca:ca + FF_CHUNK] = _gelu_gate(ua, ub).astype(BF16)

    y = _dot(f_ref[...], wdn_ref[0])
    out_ref[...] = x + _rms(y, g2_ref[0])


def _ffn(x2, g1, w_up, cw, cb, w_dn, g2, layer, s):
    t = x2.shape[0]
    assert s % FF_TM == 0
    tiles_per_seq = s // FF_TM
    per = FF_TM // SUB
    nh = t // SUB
    row = lambda i: (i, 0)
    return pl.pallas_call(
        functools.partial(_ffn_kernel, tiles_per_seq=tiles_per_seq),
        grid=(t // FF_TM,),
        in_specs=[
            pl.BlockSpec((FF_TM, D_MODEL), row),
            pl.BlockSpec((SUB, D_MODEL), lambda i: (jnp.maximum(i * per - 1, 0), 0)),
            pl.BlockSpec((SUB, D_MODEL), lambda i: (jnp.minimum((i + 1) * per, nh - 1), 0)),
            _layer_param((1, D_MODEL), layer),
            _layer_param((D_MODEL, 2 * D_FF), layer),
            _layer_param((3, 2 * D_FF), layer),
            _layer_param((1, 2 * D_FF), layer),
            _layer_param((D_FF, D_MODEL), layer),
            _layer_param((1, D_MODEL), layer),
        ],
        out_specs=pl.BlockSpec((FF_TM, D_MODEL), row),
        out_shape=jax.ShapeDtypeStruct((t, D_MODEL), F32),
        scratch_shapes=[pltpu.VMEM((FF_TM + 2 * SUB, D_MODEL), BF16),
                        pltpu.VMEM((FF_TM, D_FF), BF16)],
        compiler_params=_params(),
        name="ffn",
    )(x2, x2, x2, g1, w_up, cw, cb, w_dn, g2)


def kernel(x, mem, mem_norm_g, g_pre_mix, w_in, rpb_na, w_mem_kv, b_gate, w_br_a, w_br_b, w_br_m,
           w_out, g_post_mix, g_pre_ffn, w_up, conv_w, conv_b, w_down, g_post_ffn):
    b, s, d = x.shape
    depth = w_in.shape[0]
    assert d == D_MODEL and s % GRID_W == 0 and s % MG_TM == 0
    rows = s // GRID_W
    assert rows >= NA_KH and rows % NA_ROWS_PER_STEP == 0

    bf = lambda a: a.astype(BF16)
    vec = lambda a: a.reshape(depth, 1, -1).astype(F32)

    w_in_first = bf(w_in[:1])
    late_casts = [(w_up.reshape(depth * D_MODEL, 2 * D_FF), 0),
                  (w_down.reshape(depth * D_FF, D_MODEL), 0),
                  (w_out.reshape(depth * D_MODEL, D_MODEL), 0)]
    if depth > 1:
        late_casts.append((w_in.reshape(depth * D_MODEL, D_IN), D_MODEL))
    w_br_a, w_br_b, w_br_m = bf(w_br_a), bf(w_br_b), bf(w_br_m)
    g_pre_mix, g_post_mix, g_pre_ffn, g_post_ffn = map(vec, (g_pre_mix, g_post_mix, g_pre_ffn,
                                                             g_post_ffn))
    half = jnp.concatenate([jnp.ones((D_FF,), F32), jnp.full((D_FF,), 0.5, F32)])
    b_gate, conv_b, conv_w = vec(b_gate), vec(conv_b * half), conv_w.astype(F32) * half
    assert rpb_na.shape[1:] == (NA_HEADS, 2 * NA_KH - 1, 2 * NA_KW - 1)
    rp = jnp.pad(rpb_na.astype(F32) * LOG2E, ((0, 0), (0, 0), (0, 0),
                                      (NA_RPB_PAD, LANES - NA_RPB_PAD - (2 * NA_KW - 1))))
    kv = _memkv(mem, mem_norm_g.reshape(1, -1).astype(F32), bf(w_mem_kv))

    x2 = x.reshape(b * s, d)
    for l in range(depth):
        if l == 0:
            *proj, w_up, w_down, w_out, w_in_rest = (*_inproj(
                x2, g_pre_mix, w_in_first, b_gate, 0, 0, b, s, IN_TM_FIRST, late_casts),
                *([None] * (depth == 1)))
            w_up = w_up.reshape(depth, D_MODEL, 2 * D_FF)
            w_down = w_down.reshape(depth, D_FF, D_MODEL)
            w_out = w_out.reshape(depth, D_MODEL, D_MODEL)
            if depth > 1:
                w_in_rest = w_in_rest.reshape(depth - 1, D_MODEL, D_IN)
        else:
            proj = _inproj(x2, g_pre_mix, w_in_rest, b_gate, l - 1, l, b, s, IN_TM)
        qa, ka, va, b0, b1, b2, qm, gates = proj
        oa = _na_attention(qa, ka, va, rp, l, b, s)
        ob, lse = zip(*(_dil_attention(q, grp) for grp, q in enumerate((b0, b1, b2))))
        x2 = _merge(x2, oa, ob, lse, qm, gates, kv, w_br_a, w_br_b, w_br_m, w_out, g_post_mix, l, s)
        x2 = _ffn(x2, g_pre_ffn, w_up, conv_w, conv_b, w_down, g_post_ffn, l, s)
    return x2.reshape(b, s, d)
```

```python
import functools

import jax
import jax.numpy as jnp
from jax import lax
from jax.experimental import pallas as pl
from jax.experimental.pallas import tpu as pltpu

D_MODEL = 1024
GRID_W = 64
N_MEM = 256
HEAD_DIM = 64
NA_HEADS = 6
NA_KH = 8
NA_KW = 16
DIL_PAIRS = ((128, 1), (512, 4), (2048, 16))
DIL_HEADS = 6
MEM_HEADS = 4
D_A = 384
D_B = 384
D_B_OUT = 128
D_M = 256
N_BRANCH = 3
D_IN = 3 * D_A + 3 * D_B + D_M + N_BRANCH * D_MODEL
D_FF = 2816
RMS_EPS = 1e-6
NEG_INF = -1e30

LANES = 128
SUB = 8
SUB_BF16 = 16
VMEM_LIMIT = 56 * 1024 * 1024
IN_TM = 1024
IN_TM_FIRST = 512
MG_TM = 1024
MG_SUB = 256
FF_TM = 1024

F32 = jnp.float32
BF16 = jnp.bfloat16

LOG2E = 1.4426950408889634
LN2 = 0.6931471805599453
Q_SCALE = HEAD_DIM ** -0.5 * LOG2E


def _rms(x, g):
    return x * lax.rsqrt(jnp.mean(x * x, axis=-1, keepdims=True) + RMS_EPS) * g


def _dot(a, b):
    return jnp.dot(a, b, preferred_element_type=F32)


def _dot_nt(a, b):
    return lax.dot_general(a, b, (((1,), (1,)), ((), ())), preferred_element_type=F32)


def _layer_param(shape, layer):
    zeros = (0,) * len(shape)
    return pl.BlockSpec((1,) + tuple(shape), lambda *_: (layer,) + zeros,
                        pipeline_mode=pl.Buffered(1))


def _params(n_axes=1):
    return pltpu.CompilerParams(dimension_semantics=("arbitrary",) * n_axes,
                                vmem_limit_bytes=VMEM_LIMIT)


def _memkv_kernel(mem_ref, g_ref, w_ref, o_ref):
    mn = _rms(mem_ref[0], g_ref[...]).astype(BF16)
    o_ref[0, 0] = _dot(mn, w_ref[0]).astype(BF16)


def _memkv(mem, g, w_kv):
    depth = w_kv.shape[0]
    b = mem.shape[0]
    return pl.pallas_call(
        _memkv_kernel,
        grid=(depth, b),
        in_specs=[
            pl.BlockSpec((1, N_MEM, D_MODEL), lambda l, i: (i, 0, 0)),
            pl.BlockSpec((1, D_MODEL), lambda l, i: (0, 0)),
            pl.BlockSpec((1, D_MODEL, 2 * D_M), lambda l, i: (l, 0, 0)),
        ],
        out_specs=pl.BlockSpec((1, 1, N_MEM, 2 * D_M), lambda l, i: (l, i, 0, 0)),
        out_shape=jax.ShapeDtypeStruct((depth, b, N_MEM, 2 * D_M), BF16),
        compiler_params=_params(2),
        name="memkv",
    )(mem, g, w_kv)


IN_CHUNK = 512
_A_SLABS = D_A // LANES
_M_SLABS = D_M // LANES
_N_GROUPS = len(DIL_PAIRS)


def _inproj_kernel(*refs, n_casts):
    x_ref, g_ref, w_ref, bg_ref = refs[:4]
    cast_in = refs[4:4 + n_casts]
    qa_ref, ka_ref, va_ref, b0_ref, b1_ref, b2_ref, qm_ref, gate_ref = refs[4 + n_casts:12 + n_casts]
    cast_out = refs[12 + n_casts:12 + 2 * n_casts]
    stage_ref = refs[-1]
    tm = x_ref.shape[0]
    x = x_ref[...]
    h = (x * g_ref[0]).astype(BF16)
    inv = jnp.broadcast_to(lax.rsqrt(jnp.mean(x * x, axis=-1, keepdims=True) + RMS_EPS),
                           (tm, LANES))
    inv_q = inv * Q_SCALE
    b_refs = (b0_ref, b1_ref, b2_ref)

    def put_dilated(which, grp, val):
        d = DIL_PAIRS[grp][1]
        cs = slice(which * LANES, (which + 1) * LANES)
        if d == 1:
            b_refs[grp][0, 0, :, cs] = val.astype(BF16)
            return
        stage = stage_ref.at[(grp - 1) * 3 + which]
        stage[...] = val
        for r in range(d):
            b_refs[grp][0, r, :, cs] = stage[pl.ds(r, tm // d, stride=d), :].astype(BF16)

    def put(slab, val):
        s = slab
        if s < 3 * _A_SLABS:
            which, j = divmod(s, _A_SLABS)
            ref = (qa_ref, ka_ref, va_ref)[which]
            ref[:, j * LANES:(j + 1) * LANES] = (val * (inv_q if which == 0 else inv)).astype(BF16)
            return
        s -= 3 * _A_SLABS
        if s < 3 * _N_GROUPS:
            which, grp = divmod(s, _N_GROUPS)
            put_dilated(which, grp, val * (inv_q if which == 0 else inv))
            return
        s -= 3 * _N_GROUPS
        if s < _M_SLABS:
            qm_ref[:, s * LANES:(s + 1) * LANES] = (val * inv_q).astype(BF16)
            return
        s -= _M_SLABS
        gate = jax.nn.sigmoid(val * inv + bg_ref[0, :, s * LANES:(s + 1) * LANES])
        gate_ref[:, s * LANES:(s + 1) * LANES] = gate.astype(BF16)

    per = IN_CHUNK // LANES
    for c in reversed(range(D_IN // IN_CHUNK)):
        r = _dot(h, w_ref[0, :, c * IN_CHUNK:(c + 1) * IN_CHUNK])
        for j in range(per):
            put(c * per + j, r[:, j * LANES:(j + 1) * LANES])

    for src, dst in zip(cast_in, cast_out):
        dst[...] = src[...].astype(BF16)


def _inproj(x2, g, w_in, b_gate, layer, g_layer, b, s, tm, casts=()):
    t = x2.shape[0]
    assert s % tm == 0
    tiles_per_seq = s // tm
    steps = t // tm
    row = lambda i: (i, 0)
    outs = [jax.ShapeDtypeStruct((t, D_A), BF16)] * 3
    out_specs = [pl.BlockSpec((tm, D_A), row)] * 3
    for _, d in DIL_PAIRS:
        assert tm % (d * SUB_BF16) == 0
        outs.append(jax.ShapeDtypeStruct((b, d, s // d, 3 * LANES), BF16))
        out_specs.append(pl.BlockSpec((1, d, tm // d, 3 * LANES),
                                      lambda i: (i // tiles_per_seq, 0, i % tiles_per_seq, 0)))
    outs += [jax.ShapeDtypeStruct((t, D_M), BF16),
             jax.ShapeDtypeStruct((t, N_BRANCH * D_MODEL), BF16)]
    out_specs += [pl.BlockSpec((tm, D_M), row),
                  pl.BlockSpec((tm, N_BRANCH * D_MODEL), row)]
    cast_in_specs, cast_out_specs = [], []
    for a, skip in casts:
        rows, cols = a.shape
        blk = (rows - skip) // steps
        assert (rows - skip) % (steps * SUB_BF16) == 0 and skip % blk == 0
        cast_in_specs.append(pl.BlockSpec((blk, cols), lambda i, o=skip // blk: (o + i, 0)))
        cast_out_specs.append(pl.BlockSpec((blk, cols), row))
        outs.append(jax.ShapeDtypeStruct((rows - skip, cols), BF16))
    return pl.pallas_call(
        functools.partial(_inproj_kernel, n_casts=len(casts)),
        grid=(steps,),
        in_specs=[
            pl.BlockSpec((tm, D_MODEL), row),
            _layer_param((1, D_MODEL), g_layer),
            _layer_param((D_MODEL, D_IN), layer),
            _layer_param((1, N_BRANCH * D_MODEL), g_layer),
            *cast_in_specs,
        ],
        out_specs=out_specs + cast_out_specs,
        out_shape=outs,
        scratch_shapes=[pltpu.VMEM((3 * (_N_GROUPS - 1), tm, LANES), F32)],
        compiler_params=_params(),
        name="inproj",
    )(x2, g, w_in, b_gate, *(a for a, _ in casts))


def _pair_scores(q2, k2):
    lane = lax.broadcasted_iota(jnp.int32, q2.shape, 1)
    zero = jnp.zeros_like(q2)
    qs = jnp.concatenate([jnp.where(lane < HEAD_DIM, q2, zero),
                          jnp.where(lane < HEAD_DIM, zero, q2)], axis=0)
    return _dot_nt(qs, k2)


def _softmax_parts(s):
    m = jnp.max(s, axis=-1, keepdims=True)
    return m, jnp.exp2(s - m).astype(BF16)


def _pv_and_den(p, v2):
    r = _dot(p, jnp.concatenate([v2, jnp.ones_like(v2)], axis=1))
    return r[:, :LANES], r[:, LANES:]


def _unstack_heads(a):
    m_rows = a.shape[0] // 2
    lane = lax.broadcasted_iota(jnp.int32, (m_rows, LANES), 1)
    return jnp.where(lane < HEAD_DIM, a[:m_rows], a[m_rows:])


NA_ROWS_PER_STEP = 16
NA_GROUP = 16
NA_BAND = NA_KH * GRID_W


NA_RPB_PAD = GRID_W - NA_KW


def _na_build_bias(rp_ref, toe_ref, tab_ref):
    lane = lax.broadcasted_iota(jnp.int32, (GRID_W, LANES), 1)
    q = lax.broadcasted_iota(jnp.int32, (GRID_W, LANES), 0)
    c = lane % GRID_W
    c0 = jnp.clip(q - NA_KW // 2, 0, GRID_W - NA_KW)
    valid = (c >= c0) & (c < c0 + NA_KW)
    base_lo, base_hi = LANES - (GRID_W - 1), LANES - (GRID_W - 1) - GRID_W

    def toe_body(ro, carry):
        for h in range(NA_HEADS):
            row = jnp.broadcast_to(rp_ref[0, h, pl.ds(ro, 1), :], (GRID_W, LANES))
            lo = pltpu.roll(row, base_lo, 1, stride=1, stride_axis=0)
            hi = pltpu.roll(row, base_hi, 1, stride=1, stride_axis=0)
            toe_ref[h, ro] = jnp.where(valid, jnp.where(lane < GRID_W, lo, hi), NEG_INF)
        return carry

    lax.fori_loop(0, 2 * NA_KH - 1, toe_body, 0)

    def tab_body(dl, carry):
        for h in range(NA_HEADS):
            rows_ = slice((h % 2) * GRID_W, (h % 2 + 1) * GRID_W)
            for g in range(NA_KH // 2):
                even, odd = toe_ref[h, dl + 2 * g], toe_ref[h, dl + 2 * g + 1]
                tab_ref[h // 2, dl, rows_, g * LANES:(g + 1) * LANES] = \
                    jnp.where(lane < GRID_W, even, odd)
        return carry

    lax.fori_loop(0, NA_KH, tab_body, 0)


def _na_kernel(q_ref, k_ref, v_ref, rp_ref, o_ref, toe_ref, tab_ref, *, rows):
    rb = pl.program_id(1)

    @pl.when((pl.program_id(0) == 0) & (rb == 0))
    def _():
        _na_build_bias(rp_ref, toe_ref, tab_ref)

    def group_body(gi, carry):
        chains = []
        for lg in range(NA_GROUP):
            lr = gi * NA_GROUP + lg
            i = rb * NA_ROWS_PER_STEP + lr
            r0 = jnp.clip(i - NA_KH // 2, 0, rows - NA_KH)
            ks = pl.multiple_of(r0 * GRID_W, GRID_W)
            qs = pl.multiple_of(lr * GRID_W, GRID_W)
            for hp in range(NA_HEADS // 2):
                chains.append((qs, ks, r0 - i + (NA_KH - 1), hp,
                               slice(hp * LANES, (hp + 1) * LANES)))
        ss = [_pair_scores(q_ref[0, pl.ds(qs, GRID_W), cs], k_ref[0, pl.ds(ks, NA_BAND), cs])
              for qs, ks, dl, hp, cs in chains]
        parts = [_softmax_parts(s + tab_ref[hp, dl])
                 for s, (qs, ks, dl, hp, cs) in zip(ss, chains)]
        for (_, p), (qs, ks, dl, hp, cs) in zip(parts, chains):
            pv, den = _pv_and_den(p, v_ref[0, pl.ds(ks, NA_BAND), cs])
            o_ref[0, pl.ds(qs, GRID_W), cs] = _unstack_heads(pv / den).astype(BF16)
        return carry

    lax.fori_loop(0, NA_ROWS_PER_STEP // NA_GROUP, group_body, 0)


def _na_attention(q, k, v, rp, layer, b, s):
    rows = s // GRID_W
    qblk = NA_ROWS_PER_STEP * GRID_W
    q3, k3, v3 = (a.reshape(b, s, D_A) for a in (q, k, v))
    out = pl.pallas_call(
        functools.partial(_na_kernel, rows=rows),
        grid=(b, rows // NA_ROWS_PER_STEP),
        in_specs=[
            pl.BlockSpec((1, qblk, D_A), lambda i, j: (i, j, 0)),
            pl.BlockSpec((1, s, D_A), lambda i, j: (i, 0, 0)),
            pl.BlockSpec((1, s, D_A), lambda i, j: (i, 0, 0)),
            _layer_param(rp.shape[1:], layer),
        ],
        out_specs=pl.BlockSpec((1, qblk, D_A), lambda i, j: (i, j, 0)),
        out_shape=jax.ShapeDtypeStruct((b, s, D_A), BF16),
        scratch_shapes=[pltpu.VMEM((NA_HEADS, 2 * NA_KH - 1, GRID_W, LANES), F32),
                        pltpu.VMEM((NA_HEADS // 2, NA_KH, 2 * GRID_W, NA_BAND), F32)],
        compiler_params=_params(2),
        name="na_attn",
    )(q3, k3, v3, rp)
    return out.reshape(b * s, D_A)


DIL_QB = 128
DIL_HALF = 64
DIL_KB = DIL_QB + 2 * DIL_HALF
DIL_GROUP = 32
DIL_KEY_SHIFTS = (0, -1, -2)


def _alibi_slope(head):
    return 2.0 ** (-8.0 * (head + 1) / DIL_HEADS)


def _dil_kernel(x_ref, o_ref, lse_ref, bias_ref, *, length, dilation, slopes, residues):
    nblk = length // DIL_QB

    @pl.when((pl.program_id(0) == 0) & (pl.program_id(1) == 0))
    def _():
        rel0 = (lax.broadcasted_iota(jnp.int32, (DIL_QB, DIL_KB), 1)
                - lax.broadcasted_iota(jnp.int32, (DIL_QB, DIL_KB), 0))
        for c, shift in enumerate(DIL_KEY_SHIFTS):
            dist = jnp.abs(rel0 + shift * DIL_HALF)
            dist_f = (dist * dilation).astype(F32)
            for hh in range(2):
                bias = jnp.where(dist <= DIL_HALF, (-slopes[hh] * LOG2E) * dist_f, NEG_INF)
                bias_ref[c, hh * DIL_QB:(hh + 1) * DIL_QB, :] = bias

    def group(chains):
        geo = []
        for r, bi in chains:
            q0 = bi * DIL_QB
            if isinstance(bi, int):
                ks = min(max(q0 - DIL_HALF, 0), length - DIL_KB)
            else:
                q0 = pl.multiple_of(q0, DIL_QB)
                ks = pl.multiple_of(jnp.clip(q0 - DIL_HALF, 0, length - DIL_KB), DIL_HALF)
            geo.append((r, q0, ks, (q0 - ks) // DIL_HALF))
        ss = [_pair_scores(x_ref[0, r, pl.ds(q0, DIL_QB), 0:LANES],
                           x_ref[0, r, pl.ds(ks, DIL_KB), LANES:2 * LANES])
              for r, q0, ks, case in geo]
        parts = [_softmax_parts(s + bias_ref[case]) for s, (r, q0, ks, case) in zip(ss, geo)]
        for (m, p), (r, q0, ks, case) in zip(parts, geo):
            pv, den = _pv_and_den(p, x_ref[0, r, pl.ds(ks, DIL_KB), 2 * LANES:3 * LANES])
            o_ref[0, r, pl.ds(q0, DIL_QB), :] = _unstack_heads(pv / den).astype(BF16)
            lse_ref[0, r, pl.ds(q0, DIL_QB), :] = _unstack_heads(m * LN2 + jnp.log(den))

    if residues * nblk <= DIL_GROUP:
        group([(r, bi) for r in range(residues) for bi in range(nblk)])
    else:
        assert nblk % DIL_GROUP == 0
        for r in range(residues):
            def body(gi, carry, r=r):
                group([(r, gi * DIL_GROUP + j) for j in range(DIL_GROUP)])
                return carry
            lax.fori_loop(0, nblk // DIL_GROUP, body, 0)


def _dil_attention(qkv, grp):
    window, dilation = DIL_PAIRS[grp]
    assert window // 2 // dilation == DIL_HALF
    b, _, length, width = qkv.shape
    assert length % DIL_QB == 0 and length >= DIL_KB
    nblk = length // DIL_QB
    residues = min(dilation, max(1, DIL_GROUP // nblk))
    assert dilation % residues == 0
    slopes = tuple(_alibi_slope(2 * grp + hh) for hh in range(2))
    blk = lambda w: pl.BlockSpec((1, residues, length, w), lambda i, r: (i, r, 0, 0))
    return pl.pallas_call(
        functools.partial(_dil_kernel, length=length, dilation=dilation, slopes=slopes,
                          residues=residues),
        grid=(b, dilation // residues),
        in_specs=[blk(width)],
        out_specs=[blk(LANES), blk(LANES)],
        out_shape=[jax.ShapeDtypeStruct((b, dilation, length, LANES), BF16),
                   jax.ShapeDtypeStruct((b, dilation, length, LANES), F32)],
        scratch_shapes=[pltpu.VMEM((len(DIL_KEY_SHIFTS), 2 * DIL_QB, DIL_KB), F32)],
        compiler_params=_params(2),
        name=f"dil_attn_{grp}",
    )(qkv)


def _merge_kernel(x_ref, oa_ref, o0_ref, o1_ref, o2_ref, l0_ref, l1_ref, l2_ref, qm_ref, gate_ref,
                  kv_ref, wa_ref, wb_ref, wm_ref, wo_ref, g_ref, out_ref, stage_ref):
    blocks = [slice(r0, r0 + MG_SUB) for r0 in range(0, MG_TM, MG_SUB)]
    pairs = range(MEM_HEADS // 2)

    def natural_order(ref, grp, slot):
        d = DIL_PAIRS[grp][1]
        if d == 1:
            return ref[0, 0].astype(F32)
        stage = stage_ref.at[slot]
        for r in range(d):
            stage[pl.ds(r, MG_TM // d, stride=d), :] = ref[0, r].astype(F32)
        return stage[...]

    o_refs, l_refs = (o0_ref, o1_ref, o2_ref), (l0_ref, l1_ref, l2_ref)
    os_ = [natural_order(o_refs[g], g, 2 * (g - 1)) for g in range(_N_GROUPS)]
    ls = [natural_order(l_refs[g], g, 2 * (g - 1) + 1) for g in range(_N_GROUPS)]
    o_b = []
    for rows in blocks:
        l0, l1, l2 = (l[rows] for l in ls)
        mx = jnp.maximum(jnp.maximum(l0, l1), l2)
        e0, e1, e2 = jnp.exp(l0 - mx), jnp.exp(l1 - mx), jnp.exp(l2 - mx)
        inv = 1.0 / (e0 + e1 + e2)
        o_b.append(((e0 * inv) * os_[0][rows] + (e1 * inv) * os_[1][rows]
                    + (e2 * inv) * os_[2][rows]).astype(BF16))

    proj_b = [_dot(o, wb_ref[0]) for o in o_b]
    ss = [[_pair_scores(qm_ref[rows, hp * LANES:(hp + 1) * LANES],
                        kv_ref[0, 0, :, hp * LANES:(hp + 1) * LANES]) for hp in pairs]
          for rows in blocks]
    proj_a = [_dot(oa_ref[rows, :], wa_ref[0]) for rows in blocks]

    o_m = []
    for blk_ss in ss:
        om = []
        for hp, s in zip(pairs, blk_ss):
            _, p = _softmax_parts(s)
            pv, den = _pv_and_den(p, kv_ref[0, 0, :, D_M + hp * LANES:D_M + (hp + 1) * LANES])
            om.append(_unstack_heads(pv / den).astype(BF16))
        o_m.append(jnp.concatenate(om, axis=-1))
    proj_m = [_dot(o, wm_ref[0]) for o in o_m]

    merged = [(gate_ref[rows, 0:D_MODEL].astype(F32) * pa
               + gate_ref[rows, D_MODEL:2 * D_MODEL].astype(F32) * pb
               + gate_ref[rows, 2 * D_MODEL:3 * D_MODEL].astype(F32) * pm).astype(BF16)
              for rows, pa, pb, pm in zip(blocks, proj_a, proj_b, proj_m)]
    for rows, mg in zip(blocks, merged):
        y = _dot(mg, wo_ref[0])
        out_ref[rows, :] = x_ref[rows, :] + _rms(y, g_ref[0])


def _merge(x2, oa, ob, lse, qm, gates, kv, wa, wb, wm, wo, g, layer, s):
    t = x2.shape[0]
    tiles_per_seq = s // MG_TM
    row = lambda i: (i, 0)
    dil = [pl.BlockSpec((1, d, MG_TM // d, LANES),
                        lambda i: (i // tiles_per_seq, 0, i % tiles_per_seq, 0))
           for _, d in DIL_PAIRS]
    in_specs = [
        pl.BlockSpec((MG_TM, D_MODEL), row),
        pl.BlockSpec((MG_TM, D_A), row),
        *dil, *dil,
        pl.BlockSpec((MG_TM, D_M), row),
        pl.BlockSpec((MG_TM, N_BRANCH * D_MODEL), row),
        pl.BlockSpec((1, 1, N_MEM, 2 * D_M), lambda i: (layer, i // tiles_per_seq, 0, 0)),
        _layer_param((D_A, D_MODEL), layer), _layer_param((D_B_OUT, D_MODEL), layer),
        _layer_param((D_M, D_MODEL), layer), _layer_param((D_MODEL, D_MODEL), layer),
        _layer_param((1, D_MODEL), layer),
    ]
    return pl.pallas_call(
        _merge_kernel,
        grid=(t // MG_TM,),
        in_specs=in_specs,
        out_specs=pl.BlockSpec((MG_TM, D_MODEL), row),
        out_shape=jax.ShapeDtypeStruct((t, D_MODEL), F32),
        scratch_shapes=[pltpu.VMEM((2 * (_N_GROUPS - 1), MG_TM, LANES), F32)],
        compiler_params=_params(),
        name="merge",
    )(x2, oa, *ob, *lse, qm, gates, kv, wa, wb, wm, wo, g)


FF_CHUNK = 256
FF_FIRST_ROWS = 256
GELU_C0 = 0.7978845608028654
GELU_C1 = GELU_C0 * 0.044715


def _gelu_gate(a, half_b):
    t = jnp.tanh(a * (GELU_C0 + GELU_C1 * (a * a)))
    hb = a * half_b
    return hb + hb * t


def _ffn_kernel(xm_ref, xp_ref, xn_ref, g1_ref, wup_ref, cw_ref, cb_ref, wdn_ref, g2_ref, out_ref,
                hext_ref, f_ref, *, tiles_per_seq):
    i = pl.program_id(0)
    pos = i % tiles_per_seq
    g1 = g1_ref[0]
    x = xm_ref[...]
    hext_ref[0:FF_TM, :] = _rms(x, g1).astype(BF16)
    hn = jnp.where(pos == tiles_per_seq - 1, 0.0, _rms(xn_ref[...], g1))
    hp = jnp.where(pos == 0, 0.0, _rms(xp_ref[...], g1))
    hext_ref[FF_TM:FF_TM + 2 * SUB, :] = jnp.concatenate([hn, hp], axis=0).astype(BF16)
    hext = hext_ref[...]
    ext = FF_TM + 2 * SUB

    def conv(u, col):
        cs = slice(col, col + FF_CHUNK)
        before = pltpu.roll(u, 1, 0)[0:FF_TM]
        after = pltpu.roll(u, ext - 1, 0)[0:FF_TM]
        y = cb_ref[0, :, cs] + before * cw_ref[0, 0:1, cs]
        y = y + u[0:FF_TM] * cw_ref[0, 1:2, cs]
        return y + after * cw_ref[0, 2:3, cs]

    def up(col, first):
        w = wup_ref[0, :, col:col + FF_CHUNK]
        if not first:
            return _dot(hext, w)
        edges = list(range(0, FF_TM, FF_FIRST_ROWS)) + [ext]
        return jnp.concatenate([_dot(hext_ref[a:b, :], w) for a, b in zip(edges, edges[1:])],
                               axis=0)

    for c in range(D_FF // FF_CHUNK):
        ca, cb = c * FF_CHUNK, D_FF + c * FF_CHUNK
        ua = conv(up(ca, c == 0), ca)
        ub = conv(up(cb, False), cb)
        f_ref[:, ca:ca + FF_CHUNK] = _gelu_gate(ua, ub).astype(BF16)

    y = _dot(f_ref[...], wdn_ref[0])
    out_ref[...] = x + _rms(y, g2_ref[0])


def _ffn(x2, g1, w_up, cw, cb, w_dn, g2, layer, s):
    t = x2.shape[0]
    assert s % FF_TM == 0
    tiles_per_seq = s // FF_TM
    per = FF_TM // SUB
    nh = t // SUB
    row = lambda i: (i, 0)
    return pl.pallas_call(
        functools.partial(_ffn_kernel, tiles_per_seq=tiles_per_seq),
        grid=(t // FF_TM,),
        in_specs=[
            pl.BlockSpec((FF_TM, D_MODEL), row),
            pl.BlockSpec((SUB, D_MODEL), lambda i: (jnp.maximum(i * per - 1, 0), 0)),
            pl.BlockSpec((SUB, D_MODEL), lambda i: (jnp.minimum((i + 1) * per, nh - 1), 0)),
            _layer_param((1, D_MODEL), layer),
            _layer_param((D_MODEL, 2 * D_FF), layer),
            _layer_param((3, 2 * D_FF), layer),
            _layer_param((1, 2 * D_FF), layer),
            _layer_param((D_FF, D_MODEL), layer),
            _layer_param((1, D_MODEL), layer),
        ],
        out_specs=pl.BlockSpec((FF_TM, D_MODEL), row),
        out_shape=jax.ShapeDtypeStruct((t, D_MODEL), F32),
        scratch_shapes=[pltpu.VMEM((FF_TM + 2 * SUB, D_MODEL), BF16),
                        pltpu.VMEM((FF_TM, D_FF), BF16)],
        compiler_params=_params(),
        name="ffn",
    )(x2, x2, x2, g1, w_up, cw, cb, w_dn, g2)


def kernel(x, mem, mem_norm_g, g_pre_mix, w_in, rpb_na, w_mem_kv, b_gate, w_br_a, w_br_b, w_br_m,
           w_out, g_post_mix, g_pre_ffn, w_up, conv_w, conv_b, w_down, g_post_ffn):
    b, s, d = x.shape
    depth = w_in.shape[0]
    assert d == D_MODEL and s % GRID_W == 0 and s % MG_TM == 0
    rows = s // GRID_W
    assert rows >= NA_KH and rows % NA_ROWS_PER_STEP == 0

    bf = lambda a: a.astype(BF16)
    vec = lambda a: a.reshape(depth, 1, -1).astype(F32)

    w_in_first = bf(w_in[:1])
    late_casts = [(w_up.reshape(depth * D_MODEL, 2 * D_FF), 0),
                  (w_down.reshape(depth * D_FF, D_MODEL), 0),
                  (w_out.reshape(depth * D_MODEL, D_MODEL), 0)]
    if depth > 1:
        late_casts.append((w_in.reshape(depth * D_MODEL, D_IN), D_MODEL))
    w_br_a, w_br_b, w_br_m = bf(w_br_a), bf(w_br_b), bf(w_br_m)
    g_pre_mix, g_post_mix, g_pre_ffn, g_post_ffn = map(vec, (g_pre_mix, g_post_mix, g_pre_ffn,
                                                             g_post_ffn))
    half = jnp.concatenate([jnp.ones((D_FF,), F32), jnp.full((D_FF,), 0.5, F32)])
    b_gate, conv_b, conv_w = vec(b_gate), vec(conv_b * half), conv_w.astype(F32) * half
    assert rpb_na.shape[1:] == (NA_HEADS, 2 * NA_KH - 1, 2 * NA_KW - 1)
    rp = jnp.pad(rpb_na.astype(F32) * LOG2E, ((0, 0), (0, 0), (0, 0),
                                      (NA_RPB_PAD, LANES - NA_RPB_PAD - (2 * NA_KW - 1))))
    kv = _memkv(mem, mem_norm_g.reshape(1, -1).astype(F32), bf(w_mem_kv))

    x2 = x.reshape(b * s, d)
    for l in range(depth):
        if l == 0:
            *proj, w_up, w_down, w_out, w_in_rest = (*_inproj(
                x2, g_pre_mix, w_in_first, b_gate, 0, 0, b, s, IN_TM_FIRST, late_casts),
                *([None] * (depth == 1)))
            w_up = w_up.reshape(depth, D_MODEL, 2 * D_FF)
            w_down = w_down.reshape(depth, D_FF, D_MODEL)
            w_out = w_out.reshape(depth, D_MODEL, D_MODEL)
            if depth > 1:
                w_in_rest = w_in_rest.reshape(depth - 1, D_MODEL, D_IN)
        else:
            proj = _inproj(x2, g_pre_mix, w_in_rest, b_gate, l - 1, l, b, s, IN_TM)
        qa, ka, va, b0, b1, b2, qm, gates = proj
        oa = _na_attention(qa, ka, va, rp, l, b, s)
        ob, lse = zip(*(_dil_attention(q, grp) for grp, q in enumerate((b0, b1, b2))))
        x2 = _merge(x2, oa, ob, lse, qm, gates, kv, w_br_a, w_br_b, w_br_m, w_out, g_post_mix, l, s)
        x2 = _ffn(x2, g_pre_ffn, w_up, conv_w, conv_b, w_down, g_post_ffn, l, s)
    return x2.reshape(b, s, d)
```

```python
import functools

import jax
import jax.numpy as jnp
from jax import lax
from jax.experimental import pallas as pl
from jax.experimental.pallas import tpu as pltpu

D_MODEL = 1024
GRID_W = 64
N_MEM = 256
HEAD_DIM = 64
NA_HEADS = 6
NA_KH = 8
NA_KW = 16
DIL_PAIRS = ((128, 1), (512, 4), (2048, 16))
DIL_HEADS = 6
MEM_HEADS = 4
D_A = 384
D_B = 384
D_B_OUT = 128
D_M = 256
N_BRANCH = 3
D_IN = 3 * D_A + 3 * D_B + D_M + N_BRANCH * D_MODEL
D_FF = 2816
RMS_EPS = 1e-6
NEG_INF = -1e30

LANES = 128
SUB = 8
SUB_BF16 = 16
VMEM_LIMIT = 56 * 1024 * 1024
IN_TM = 1024
IN_TM_FIRST = 512
MG_TM = 1024
MG_SUB = 256
FF_TM = 1024

F32 = jnp.float32
BF16 = jnp.bfloat16

LOG2E = 1.4426950408889634
LN2 = 0.6931471805599453
Q_SCALE = HEAD_DIM ** -0.5 * LOG2E


def _rms(x, g):
    return x * lax.rsqrt(jnp.mean(x * x, axis=-1, keepdims=True) + RMS_EPS) * g


def _dot(a, b):
    return jnp.dot(a, b, preferred_element_type=F32)


def _dot_nt(a, b):
    return lax.dot_general(a, b, (((1,), (1,)), ((), ())), preferred_element_type=F32)


def _layer_param(shape, layer):
    zeros = (0,) * len(shape)
    return pl.BlockSpec((1,) + tuple(shape), lambda *_: (layer,) + zeros,
                        pipeline_mode=pl.Buffered(1))


def _params(n_axes=1):
    return pltpu.CompilerParams(dimension_semantics=("arbitrary",) * n_axes,
                                vmem_limit_bytes=VMEM_LIMIT)


def _memkv_kernel(mem_ref, g_ref, w_ref, o_ref):
    mn = _rms(mem_ref[0], g_ref[...]).astype(BF16)
    o_ref[0, 0] = _dot(mn, w_ref[0]).astype(BF16)


def _memkv(mem, g, w_kv):
    depth = w_kv.shape[0]
    b = mem.shape[0]
    return pl.pallas_call(
        _memkv_kernel,
        grid=(depth, b),
        in_specs=[
            pl.BlockSpec((1, N_MEM, D_MODEL), lambda l, i: (i, 0, 0)),
            pl.BlockSpec((1, D_MODEL), lambda l, i: (0, 0)),
            pl.BlockSpec((1, D_MODEL, 2 * D_M), lambda l, i: (l, 0, 0)),
        ],
        out_specs=pl.BlockSpec((1, 1, N_MEM, 2 * D_M), lambda l, i: (l, i, 0, 0)),
        out_shape=jax.ShapeDtypeStruct((depth, b, N_MEM, 2 * D_M), BF16),
        compiler_params=_params(2),
        name="memkv",
    )(mem, g, w_kv)


IN_CHUNK = 512
_A_SLABS = D_A // LANES
_M_SLABS = D_M // LANES
_N_GROUPS = len(DIL_PAIRS)


def _inproj_kernel(*refs, n_casts):
    x_ref, g_ref, w_ref, bg_ref = refs[:4]
    cast_in = refs[4:4 + n_casts]
    qa_ref, ka_ref, va_ref, b0_ref, b1_ref, b2_ref, qm_ref, gate_ref = refs[4 + n_casts:12 + n_casts]
    cast_out = refs[12 + n_casts:12 + 2 * n_casts]
    stage_ref = refs[-1]
    tm = x_ref.shape[0]
    x = x_ref[...]
    h = (x * g_ref[0]).astype(BF16)
    inv = jnp.broadcast_to(lax.rsqrt(jnp.mean(x * x, axis=-1, keepdims=True) + RMS_EPS),
                           (tm, LANES))
    inv_q = inv * Q_SCALE
    b_refs = (b0_ref, b1_ref, b2_ref)

    def put_dilated(which, grp, val):
        d = DIL_PAIRS[grp][1]
        cs = slice(which * LANES, (which + 1) * LANES)
        if d == 1:
            b_refs[grp][0, 0, :, cs] = val.astype(BF16)
            return
        stage = stage_ref.at[(grp - 1) * 3 + which]
        stage[...] = val
        for r in range(d):
            b_refs[grp][0, r, :, cs] = stage[pl.ds(r, tm // d, stride=d), :].astype(BF16)

    def put(slab, val):
        s = slab
        if s < 3 * _A_SLABS:
            which, j = divmod(s, _A_SLABS)
            ref = (qa_ref, ka_ref, va_ref)[which]
            ref[:, j * LANES:(j + 1) * LANES] = (val * (inv_q if which == 0 else inv)).astype(BF16)
            return
        s -= 3 * _A_SLABS
        if s < 3 * _N_GROUPS:
            which, grp = divmod(s, _N_GROUPS)
            put_dilated(which, grp, val * (inv_q if which == 0 else inv))
            return
        s -= 3 * _N_GROUPS
        if s < _M_SLABS:
            qm_ref[:, s * LANES:(s + 1) * LANES] = (val * inv_q).astype(BF16)
            return
        s -= _M_SLABS
        gate = jax.nn.sigmoid(val * inv + bg_ref[0, :, s * LANES:(s + 1) * LANES])
        gate_ref[:, s * LANES:(s + 1) * LANES] = gate.astype(BF16)

    per = IN_CHUNK // LANES
    for c in reversed(range(D_IN // IN_CHUNK)):
        r = _dot(h, w_ref[0, :, c * IN_CHUNK:(c + 1) * IN_CHUNK])
        for j in range(per):
            put(c * per + j, r[:, j * LANES:(j + 1) * LANES])

    for src, dst in zip(cast_in, cast_out):
        dst[...] = src[...].astype(BF16)


def _inproj(x2, g, w_in, b_gate, layer, g_layer, b, s, tm, casts=()):
    t = x2.shape[0]
    assert s % tm == 0
    tiles_per_seq = s // tm
    steps = t // tm
    row = lambda i: (i, 0)
    outs = [jax.ShapeDtypeStruct((t, D_A), BF16)] * 3
    out_specs = [pl.BlockSpec((tm, D_A), row)] * 3
    for _, d in DIL_PAIRS:
        assert tm % (d * SUB_BF16) == 0
        outs.append(jax.ShapeDtypeStruct((b, d, s // d, 3 * LANES), BF16))
        out_specs.append(pl.BlockSpec((1, d, tm // d, 3 * LANES),
                                      lambda i: (i // tiles_per_seq, 0, i % tiles_per_seq, 0)))
    outs += [jax.ShapeDtypeStruct((t, D_M), BF16),
             jax.ShapeDtypeStruct((t, N_BRANCH * D_MODEL), BF16)]
    out_specs += [pl.BlockSpec((tm, D_M), row),
                  pl.BlockSpec((tm, N_BRANCH * D_MODEL), row)]
    cast_in_specs, cast_out_specs = [], []
    for a, skip in casts:
        rows, cols = a.shape
        blk = (rows - skip) // steps
        assert (rows - skip) % (steps * SUB_BF16) == 0 and skip % blk == 0
        cast_in_specs.append(pl.BlockSpec((blk, cols), lambda i, o=skip // blk: (o + i, 0)))
        cast_out_specs.append(pl.BlockSpec((blk, cols), row))
        outs.append(jax.ShapeDtypeStruct((rows - skip, cols), BF16))
    return pl.pallas_call(
        functools.partial(_inproj_kernel, n_casts=len(casts)),
        grid=(steps,),
        in_specs=[
            pl.BlockSpec((tm, D_MODEL), row),
            _layer_param((1, D_MODEL), g_layer),
            _layer_param((D_MODEL, D_IN), layer),
            _layer_param((1, N_BRANCH * D_MODEL), g_layer),
            *cast_in_specs,
        ],
        out_specs=out_specs + cast_out_specs,
        out_shape=outs,
        scratch_shapes=[pltpu.VMEM((3 * (_N_GROUPS - 1), tm, LANES), F32)],
        compiler_params=_params(),
        name="inproj",
    )(x2, g, w_in, b_gate, *(a for a, _ in casts))


def _pair_scores(q2, k2):
    lane = lax.broadcasted_iota(jnp.int32, q2.shape, 1)
    zero = jnp.zeros_like(q2)
    qs = jnp.concatenate([jnp.where(lane < HEAD_DIM, q2, zero),
                          jnp.where(lane < HEAD_DIM, zero, q2)], axis=0)
    return _dot_nt(qs, k2)


def _softmax_parts(s):
    m = jnp.max(s, axis=-1, keepdims=True)
    return m, jnp.exp2(s - m).astype(BF16)


def _pv_and_den(p, v2):
    r = _dot(p, jnp.concatenate([v2, jnp.ones_like(v2)], axis=1))
    return r[:, :LANES], r[:, LANES:]


def _unstack_heads(a):
    m_rows = a.shape[0] // 2
    lane = lax.broadcasted_iota(jnp.int32, (m_rows, LANES), 1)
    return jnp.where(lane < HEAD_DIM, a[:m_rows], a[m_rows:])


NA_ROWS_PER_STEP = 16
NA_GROUP = 16
NA_BAND = NA_KH * GRID_W


NA_RPB_PAD = GRID_W - NA_KW


def _na_build_bias(rp_ref, toe_ref, tab_ref):
    lane = lax.broadcasted_iota(jnp.int32, (GRID_W, LANES), 1)
    q = lax.broadcasted_iota(jnp.int32, (GRID_W, LANES), 0)
    c = lane % GRID_W
    c0 = jnp.clip(q - NA_KW // 2, 0, GRID_W - NA_KW)
    valid = (c >= c0) & (c < c0 + NA_KW)
    base_lo, base_hi = LANES - (GRID_W - 1), LANES - (GRID_W - 1) - GRID_W

    def toe_body(ro, carry):
        for h in range(NA_HEADS):
            row = jnp.broadcast_to(rp_ref[0, h, pl.ds(ro, 1), :], (GRID_W, LANES))
            lo = pltpu.roll(row, base_lo, 1, stride=1, stride_axis=0)
            hi = pltpu.roll(row, base_hi, 1, stride=1, stride_axis=0)
            toe_ref[h, ro] = jnp.where(valid, jnp.where(lane < GRID_W, lo, hi), NEG_INF)
        return carry

    lax.fori_loop(0, 2 * NA_KH - 1, toe_body, 0)

    def tab_body(dl, carry):
        for h in range(NA_HEADS):
            rows_ = slice((h % 2) * GRID_W, (h % 2 + 1) * GRID_W)
            for g in range(NA_KH // 2):
                even, odd = toe_ref[h, dl + 2 * g], toe_ref[h, dl + 2 * g + 1]
                tab_ref[h // 2, dl, rows_, g * LANES:(g + 1) * LANES] = \
                    jnp.where(lane < GRID_W, even, odd)
        return carry

    lax.fori_loop(0, NA_KH, tab_body, 0)


def _na_kernel(q_ref, k_ref, v_ref, rp_ref, o_ref, toe_ref, tab_ref, *, rows):
    rb = pl.program_id(1)

    @pl.when((pl.program_id(0) == 0) & (rb == 0))
    def _():
        _na_build_bias(rp_ref, toe_ref, tab_ref)

    def group_body(gi, carry):
        chains = []
        for lg in range(NA_GROUP):
            lr = gi * NA_GROUP + lg
            i = rb * NA_ROWS_PER_STEP + lr
            r0 = jnp.clip(i - NA_KH // 2, 0, rows - NA_KH)
            ks = pl.multiple_of(r0 * GRID_W, GRID_W)
            qs = pl.multiple_of(lr * GRID_W, GRID_W)
            for hp in range(NA_HEADS // 2):
                chains.append((qs, ks, r0 - i + (NA_KH - 1), hp,
                               slice(hp * LANES, (hp + 1) * LANES)))
        ss = [_pair_scores(q_ref[0, pl.ds(qs, GRID_W), cs], k_ref[0, pl.ds(ks, NA_BAND), cs])
              for qs, ks, dl, hp, cs in chains]
        parts = [_softmax_parts(s + tab_ref[hp, dl])
                 for s, (qs, ks, dl, hp, cs) in zip(ss, chains)]
        for (_, p), (qs, ks, dl, hp, cs) in zip(parts, chains):
            pv, den = _pv_and_den(p, v_ref[0, pl.ds(ks, NA_BAND), cs])
            o_ref[0, pl.ds(qs, GRID_W), cs] = _unstack_heads(pv / den).astype(BF16)
        return carry

    lax.fori_loop(0, NA_ROWS_PER_STEP // NA_GROUP, group_body, 0)


def _na_attention(q, k, v, rp, layer, b, s):
    rows = s // GRID_W
    qblk = NA_ROWS_PER_STEP * GRID_W
    q3, k3, v3 = (a.reshape(b, s, D_A) for a in (q, k, v))
    out = pl.pallas_call(
        functools.partial(_na_kernel, rows=rows),
        grid=(b, rows // NA_ROWS_PER_STEP),
        in_specs=[
            pl.BlockSpec((1, qblk, D_A), lambda i, j: (i, j, 0)),
            pl.BlockSpec((1, s, D_A), lambda i, j: (i, 0, 0)),
            pl.BlockSpec((1, s, D_A), lambda i, j: (i, 0, 0)),
            _layer_param(rp.shape[1:], layer),
        ],
        out_specs=pl.BlockSpec((1, qblk, D_A), lambda i, j: (i, j, 0)),
        out_shape=jax.ShapeDtypeStruct((b, s, D_A), BF16),
        scratch_shapes=[pltpu.VMEM((NA_HEADS, 2 * NA_KH - 1, GRID_W, LANES), F32),
                        pltpu.VMEM((NA_HEADS // 2, NA_KH, 2 * GRID_W, NA_BAND), F32)],
        compiler_params=_params(2),
        name="na_attn",
    )(q3, k3, v3, rp)
    return out.reshape(b * s, D_A)


DIL_QB = 128
DIL_HALF = 64
DIL_KB = DIL_QB + 2 * DIL_HALF
DIL_GROUP = 32
DIL_KEY_SHIFTS = (0, -1, -2)


def _alibi_slope(head):
    return 2.0 ** (-8.0 * (head + 1) / DIL_HEADS)


def _dil_kernel(x_ref, o_ref, lse_ref, bias_ref, *, length, dilation, slopes, residues):
    nblk = length // DIL_QB

    @pl.when((pl.program_id(0) == 0) & (pl.program_id(1) == 0))
    def _():
        rel0 = (lax.broadcasted_iota(jnp.int32, (DIL_QB, DIL_KB), 1)
                - lax.broadcasted_iota(jnp.int32, (DIL_QB, DIL_KB), 0))
        for c, shift in enumerate(DIL_KEY_SHIFTS):
            dist = jnp.abs(rel0 + shift * DIL_HALF)
            dist_f = (dist * dilation).astype(F32)
            for hh in range(2):
                bias = jnp.where(dist <= DIL_HALF, (-slopes[hh] * LOG2E) * dist_f, NEG_INF)
                bias_ref[c, hh * DIL_QB:(hh + 1) * DIL_QB, :] = bias

    def group(chains):
        geo = []
        for r, bi in chains:
            q0 = bi * DIL_QB
            if isinstance(bi, int):
                ks = min(max(q0 - DIL_HALF, 0), length - DIL_KB)
            else:
                q0 = pl.multiple_of(q0, DIL_QB)
                ks = pl.multiple_of(jnp.clip(q0 - DIL_HALF, 0, length - DIL_KB), DIL_HALF)
            geo.append((r, q0, ks, (q0 - ks) // DIL_HALF))
        ss = [_pair_scores(x_ref[0, r, pl.ds(q0, DIL_QB), 0:LANES],
                           x_ref[0, r, pl.ds(ks, DIL_KB), LANES:2 * LANES])
              for r, q0, ks, case in geo]
        parts = [_softmax_parts(s + bias_ref[case]) for s, (r, q0, ks, case) in zip(ss, geo)]
        for (m, p), (r, q0, ks, case) in zip(parts, geo):
            pv, den = _pv_and_den(p, x_ref[0, r, pl.ds(ks, DIL_KB), 2 * LANES:3 * LANES])
            o_ref[0, r, pl.ds(q0, DIL_QB), :] = _unstack_heads(pv / den).astype(BF16)
            lse_ref[0, r, pl.ds(q0, DIL_QB), :] = _unstack_heads(m * LN2 + jnp.log(den))

    if residues * nblk <= DIL_GROUP:
        group([(r, bi) for r in range(residues) for bi in range(nblk)])
    else:
        assert nblk % DIL_GROUP == 0
        for r in range(residues):
            def body(gi, carry, r=r):
                group([(r, gi * DIL_GROUP + j) for j in range(DIL_GROUP)])
                return carry
            lax.fori_loop(0, nblk // DIL_GROUP, body, 0)


def _dil_attention(qkv, grp):
    window, dilation = DIL_PAIRS[grp]
    assert window // 2 // dilation == DIL_HALF
    b, _, length, width = qkv.shape
    assert length % DIL_QB == 0 and length >= DIL_KB
    nblk = length // DIL_QB
    residues = min(dilation, max(1, DIL_GROUP // nblk))
    assert dilation % residues == 0
    slopes = tuple(_alibi_slope(2 * grp + hh) for hh in range(2))
    blk = lambda w: pl.BlockSpec((1, residues, length, w), lambda i, r: (i, r, 0, 0))
    return pl.pallas_call(
        functools.partial(_dil_kernel, length=length, dilation=dilation, slopes=slopes,
                          residues=residues),
        grid=(b, dilation // residues),
        in_specs=[blk(width)],
        out_specs=[blk(LANES), blk(LANES)],
        out_shape=[jax.ShapeDtypeStruct((b, dilation, length, LANES), BF16),
                   jax.ShapeDtypeStruct((b, dilation, length, LANES), F32)],
        scratch_shapes=[pltpu.VMEM((len(DIL_KEY_SHIFTS), 2 * DIL_QB, DIL_KB), F32)],
        compiler_params=_params(2),
        name=f"dil_attn_{grp}",
    )(qkv)


def _merge_kernel(x_ref, oa_ref, o0_ref, o1_ref, o2_ref, l0_ref, l1_ref, l2_ref, qm_ref, gate_ref,
                  kv_ref, wa_ref, wb_ref, wm_ref, wo_ref, g_ref, out_ref, stage_ref):
    blocks = [slice(r0, r0 + MG_SUB) for r0 in range(0, MG_TM, MG_SUB)]
    pairs = range(MEM_HEADS // 2)

    def natural_order(ref, grp, slot):
        d = DIL_PAIRS[grp][1]
        if d == 1:
            return ref[0, 0].astype(F32)
        stage = stage_ref.at[slot]
        for r in range(d):
            stage[pl.ds(r, MG_TM // d, stride=d), :] = ref[0, r].astype(F32)
        return stage[...]

    o_refs, l_refs = (o0_ref, o1_ref, o2_ref), (l0_ref, l1_ref, l2_ref)
    os_ = [natural_order(o_refs[g], g, 2 * (g - 1)) for g in range(_N_GROUPS)]
    ls = [natural_order(l_refs[g], g, 2 * (g - 1) + 1) for g in range(_N_GROUPS)]
    o_b = []
    for rows in blocks:
        l0, l1, l2 = (l[rows] for l in ls)
        mx = jnp.maximum(jnp.maximum(l0, l1), l2)
        e0, e1, e2 = jnp.exp(l0 - mx), jnp.exp(l1 - mx), jnp.exp(l2 - mx)
        inv = 1.0 / (e0 + e1 + e2)
        o_b.append(((e0 * inv) * os_[0][rows] + (e1 * inv) * os_[1][rows]
                    + (e2 * inv) * os_[2][rows]).astype(BF16))

    proj_b = [_dot(o, wb_ref[0]) for o in o_b]
    ss = [[_pair_scores(qm_ref[rows, hp * LANES:(hp + 1) * LANES],
                        kv_ref[0, 0, :, hp * LANES:(hp + 1) * LANES]) for hp in pairs]
          for rows in blocks]
    proj_a = [_dot(oa_ref[rows, :], wa_ref[0]) for rows in blocks]

    o_m = []
    for blk_ss in ss:
        om = []
        for hp, s in zip(pairs, blk_ss):
            _, p = _softmax_parts(s)
            pv, den = _pv_and_den(p, kv_ref[0, 0, :, D_M + hp * LANES:D_M + (hp + 1) * LANES])
            om.append(_unstack_heads(pv / den).astype(BF16))
        o_m.append(jnp.concatenate(om, axis=-1))
    proj_m = [_dot(o, wm_ref[0]) for o in o_m]

    merged = [(gate_ref[rows, 0:D_MODEL].astype(F32) * pa
               + gate_ref[rows, D_MODEL:2 * D_MODEL].astype(F32) * pb
               + gate_ref[rows, 2 * D_MODEL:3 * D_MODEL].astype(F32) * pm).astype(BF16)
              for rows, pa, pb, pm in zip(blocks, proj_a, proj_b, proj_m)]
    for rows, mg in zip(blocks, merged):
        y = _dot(mg, wo_ref[0])
        out_ref[rows, :] = x_ref[rows, :] + _rms(y, g_ref[0])


def _merge(x2, oa, ob, lse, qm, gates, kv, wa, wb, wm, wo, g, layer, s):
    t = x2.shape[0]
    tiles_per_seq = s // MG_TM
    row = lambda i: (i, 0)
    dil = [pl.BlockSpec((1, d, MG_TM // d, LANES),
                        lambda i: (i // tiles_per_seq, 0, i % tiles_per_seq, 0))
           for _, d in DIL_PAIRS]
    in_specs = [
        pl.BlockSpec((MG_TM, D_MODEL), row),
        pl.BlockSpec((MG_TM, D_A), row),
        *dil, *dil,
        pl.BlockSpec((MG_TM, D_M), row),
        pl.BlockSpec((MG_TM, N_BRANCH * D_MODEL), row),
        pl.BlockSpec((1, 1, N_MEM, 2 * D_M), lambda i: (layer, i // tiles_per_seq, 0, 0)),
        _layer_param((D_A, D_MODEL), layer), _layer_param((D_B_OUT, D_MODEL), layer),
        _layer_param((D_M, D_MODEL), layer), _layer_param((D_MODEL, D_MODEL), layer),
        _layer_param((1, D_MODEL), layer),
    ]
    return pl.pallas_call(
        _merge_kernel,
        grid=(t // MG_TM,),
        in_specs=in_specs,
        out_specs=pl.BlockSpec((MG_TM, D_MODEL), row),
        out_shape=jax.ShapeDtypeStruct((t, D_MODEL), F32),
        scratch_shapes=[pltpu.VMEM((2 * (_N_GROUPS - 1), MG_TM, LANES), F32)],
        compiler_params=_params(),
        name="merge",
    )(x2, oa, *ob, *lse, qm, gates, kv, wa, wb, wm, wo, g)


FF_CHUNK = 256
FF_FIRST_ROWS = 256
GELU_C0 = 0.7978845608028654
GELU_C1 = GELU_C0 * 0.044715


def _gelu_gate(a, half_b):
    t = jnp.tanh(a * (GELU_C0 + GELU_C1 * (a * a)))
    hb = a * half_b
    return hb + hb * t


def _ffn_kernel(xm_ref, xp_ref, xn_ref, g1_ref, wup_ref, cw_ref, cb_ref, wdn_ref, g2_ref, out_ref,
                hext_ref, f_ref, *, tiles_per_seq):
    i = pl.program_id(0)
    pos = i % tiles_per_seq
    g1 = g1_ref[0]
    x = xm_ref[...]
    hext_ref[0:FF_TM, :] = _rms(x, g1).astype(BF16)
    hn = jnp.where(pos == tiles_per_seq - 1, 0.0, _rms(xn_ref[...], g1))
    hp = jnp.where(pos == 0, 0.0, _rms(xp_ref[...], g1))
    hext_ref[FF_TM:FF_TM + 2 * SUB, :] = jnp.concatenate([hn, hp], axis=0).astype(BF16)
    hext = hext_ref[...]
    ext = FF_TM + 2 * SUB

    def conv(u, col):
        cs = slice(col, col + u.shape[1])
        before = pltpu.roll(u, 1, 0)[0:FF_TM]
        after = pltpu.roll(u, ext - 1, 0)[0:FF_TM]
        y = cb_ref[0, :, cs] + before * cw_ref[0, 0:1, cs]
        y = y + u[0:FF_TM] * cw_ref[0, 1:2, cs]
        return y + after * cw_ref[0, 2:3, cs]

    def up(col, first):
        w = wup_ref[0, :, col:col + FF_CHUNK]
        if not first:
            return _dot(hext, w)
        edges = list(range(0, FF_TM, FF_FIRST_ROWS)) + [ext]
        return jnp.concatenate([_dot(hext_ref[a:b, :], w) for a, b in zip(edges, edges[1:])],
                               axis=0)

    for c in range(D_FF // FF_CHUNK):
        ca, cb = c * FF_CHUNK, D_FF + c * FF_CHUNK
        u_a, u_b = up(ca, c == 0), up(cb, False)
        for h0 in range(0, FF_CHUNK, LANES):
            ua = conv(u_a[:, h0:h0 + LANES], ca + h0)
            ub = conv(u_b[:, h0:h0 + LANES], cb + h0)
            f_ref[:, ca + h0:ca + h0 + LANES] = _gelu_gate(ua, ub).astype(BF16)

    y = _dot(f_ref[...], wdn_ref[0])
    out_ref[...] = x + _rms(y, g2_ref[0])


def _ffn(x2, g1, w_up, cw, cb, w_dn, g2, layer, s):
    t = x2.shape[0]
    assert s % FF_TM == 0
    tiles_per_seq = s // FF_TM
    per = FF_TM // SUB
    nh = t // SUB
    row = lambda i: (i, 0)
    return pl.pallas_call(
        functools.partial(_ffn_kernel, tiles_per_seq=tiles_per_seq),
        grid=(t // FF_TM,),
        in_specs=[
            pl.BlockSpec((FF_TM, D_MODEL), row),
            pl.BlockSpec((SUB, D_MODEL), lambda i: (jnp.maximum(i * per - 1, 0), 0)),
            pl.BlockSpec((SUB, D_MODEL), lambda i: (jnp.minimum((i + 1) * per, nh - 1), 0)),
            _layer_param((1, D_MODEL), layer),
            _layer_param((D_MODEL, 2 * D_FF), layer),
            _layer_param((3, 2 * D_FF), layer),
            _layer_param((1, 2 * D_FF), layer),
            _layer_param((D_FF, D_MODEL), layer),
            _layer_param((1, D_MODEL), layer),
        ],
        out_specs=pl.BlockSpec((FF_TM, D_MODEL), row),
        out_shape=jax.ShapeDtypeStruct((t, D_MODEL), F32),
        scratch_shapes=[pltpu.VMEM((FF_TM + 2 * SUB, D_MODEL), BF16),
                        pltpu.VMEM((FF_TM, D_FF), BF16)],
        compiler_params=_params(),
        name="ffn",
    )(x2, x2, x2, g1, w_up, cw, cb, w_dn, g2)


def kernel(x, mem, mem_norm_g, g_pre_mix, w_in, rpb_na, w_mem_kv, b_gate, w_br_a, w_br_b, w_br_m,
           w_out, g_post_mix, g_pre_ffn, w_up, conv_w, conv_b, w_down, g_post_ffn):
    b, s, d = x.shape
    depth = w_in.shape[0]
    assert d == D_MODEL and s % GRID_W == 0 and s % MG_TM == 0
    rows = s // GRID_W
    assert rows >= NA_KH and rows % NA_ROWS_PER_STEP == 0

    bf = lambda a: a.astype(BF16)
    vec = lambda a: a.reshape(depth, 1, -1).astype(F32)

    w_in_first = bf(w_in[:1])
    late_casts = [(w_up.reshape(depth * D_MODEL, 2 * D_FF), 0),
                  (w_down.reshape(depth * D_FF, D_MODEL), 0),
                  (w_out.reshape(depth * D_MODEL, D_MODEL), 0)]
    if depth > 1:
        late_casts.append((w_in.reshape(depth * D_MODEL, D_IN), D_MODEL))
    w_br_a, w_br_b, w_br_m = bf(w_br_a), bf(w_br_b), bf(w_br_m)
    g_pre_mix, g_post_mix, g_pre_ffn, g_post_ffn = map(vec, (g_pre_mix, g_post_mix, g_pre_ffn,
                                                             g_post_ffn))
    half = jnp.concatenate([jnp.ones((D_FF,), F32), jnp.full((D_FF,), 0.5, F32)])
    b_gate, conv_b, conv_w = vec(b_gate), vec(conv_b * half), conv_w.astype(F32) * half
    assert rpb_na.shape[1:] == (NA_HEADS, 2 * NA_KH - 1, 2 * NA_KW - 1)
    rp = jnp.pad(rpb_na.astype(F32) * LOG2E, ((0, 0), (0, 0), (0, 0),
                                      (NA_RPB_PAD, LANES - NA_RPB_PAD - (2 * NA_KW - 1))))
    kv = _memkv(mem, mem_norm_g.reshape(1, -1).astype(F32), bf(w_mem_kv))

    x2 = x.reshape(b * s, d)
    for l in range(depth):
        if l == 0:
            *proj, w_up, w_down, w_out, w_in_rest = (*_inproj(
                x2, g_pre_mix, w_in_first, b_gate, 0, 0, b, s, IN_TM_FIRST, late_casts),
                *([None] * (depth == 1)))
            w_up = w_up.reshape(depth, D_MODEL, 2 * D_FF)
            w_down = w_down.reshape(depth, D_FF, D_MODEL)
            w_out = w_out.reshape(depth, D_MODEL, D_MODEL)
            if depth > 1:
                w_in_rest = w_in_rest.reshape(depth - 1, D_MODEL, D_IN)
        else:
            proj = _inproj(x2, g_pre_mix, w_in_rest, b_gate, l - 1, l, b, s, IN_TM)
        qa, ka, va, b0, b1, b2, qm, gates = proj
        oa = _na_attention(qa, ka, va, rp, l, b, s)
        ob, lse = zip(*(_dil_attention(q, grp) for grp, q in enumerate((b0, b1, b2))))
        x2 = _merge(x2, oa, ob, lse, qm, gates, kv, w_br_a, w_br_b, w_br_m, w_out, g_post_mix, l, s)
        x2 = _ffn(x2, g_pre_ffn, w_up, conv_w, conv_b, w_down, g_post_ffn, l, s)
    return x2.reshape(b, s, d)
```
